```python
import math, functools
import jax, jax.numpy as jnp
from jax import lax
import numpy as np

D_MODEL = 2048
BATCH = 1
SEQ = 8192
DEPTH = 4
DEC_BATCH = 2
DEC_SEQ = 16384
PAST_LEN = 128

HEAD_DIM = 128
MIX_WIDTH = D_MODEL
NA_HEADS = 4
NA_WIDTH = NA_HEADS * HEAD_DIM
FN_GROUPS = 4
FN_GROUP_DIM = 128
FN_WIDTH = FN_GROUPS * FN_GROUP_DIM
SWA_Q_HEADS = 8
SWA_KV_HEADS = 2
SWA_GROUP = SWA_Q_HEADS // SWA_KV_HEADS
SWA_WIDTH = SWA_Q_HEADS * HEAD_DIM
SWA_KV_WIDTH = SWA_KV_HEADS * HEAD_DIM
IN_WIDTH = 3 * NA_WIDTH + FN_WIDTH + SWA_WIDTH + 2 * SWA_KV_WIDTH
GRID_W = 64
NA_ROWS = 8
NA_COLS = 16
SWA_WINDOW = 128
SWA_BLOCK = 128
MEM_TOKENS = 256
MEM_HEADS = 4
MEM_WIDTH = MEM_HEADS * HEAD_DIM
N_EXPERTS = 16
EC_CAPACITY_FACTOR = 2
D_FF_EXPERT = 2 * D_MODEL
RMS_EPS = 1e-6
NEG_INF = -1e30

kernel_name = "hybrid_na_fnet_swa_ec_encoder"


def rmsnorm(x, g):
    xf = x.astype(jnp.float32)
    y = xf * lax.rsqrt(jnp.mean(xf * xf, axis=-1, keepdims=True) + RMS_EPS)
    return (y * g.astype(jnp.float32)).astype(x.dtype)


def alibi_slopes(n_heads):
    return jnp.exp2(-jnp.arange(1, n_heads + 1, dtype=jnp.float32) * (8.0 / n_heads))


def neighborhood_attention(q, k, v, bias_tab):
    B, S, H, dh = q.shape
    rows = S // GRID_W
    wr = min(NA_ROWS, rows)
    qg = q.reshape(B, rows, GRID_W, H, dh)
    kg = k.reshape(B, rows, GRID_W, H, dh)
    vg = v.reshape(B, rows, GRID_W, H, dh)
    r = jnp.arange(rows)
    r0 = jnp.clip(r - wr // 2, 0, rows - wr)
    krow = r0[:, None] + jnp.arange(wr)[None, :]
    kn = kg[:, krow]
    vn = vg[:, krow]
    s = jnp.einsum('brqhd,brwkhd->brhqwk', qg, kn).astype(jnp.float32) * (dh ** -0.5)
    c = jnp.arange(GRID_W)
    c0 = jnp.clip(c - NA_COLS // 2, 0, GRID_W - NA_COLS)
    in_col = (c[None, :] >= c0[:, None]) & (c[None, :] < c0[:, None] + NA_COLS)
    dc_idx = jnp.clip(c[None, :] - c[:, None], -(NA_COLS - 1), NA_COLS - 1) + NA_COLS - 1
    dr_idx = krow - r[:, None] + NA_ROWS - 1
    bias = bias_tab.astype(jnp.float32)[:, dr_idx]
    bias = bias[..., dc_idx]
    bias = bias.transpose(1, 0, 3, 2, 4)
    s = jnp.where(in_col[None, None, None, :, None, :], s + bias[None], NEG_INF)
    p = jax.nn.softmax(s, axis=(-2, -1))
    o = jnp.einsum('brhqwk,brwkhd->brqhd', p.astype(v.dtype), vn)
    return o.reshape(B, S, H, dh)


def fourier_mix(u, w):
    f = jnp.fft.fft2(u.astype(jnp.float32), axes=(1, 3), norm='ortho').real
    return jnp.einsum('bsgc,gcd->bsgd', f.astype(u.dtype), w)


def window_gqa(q, k, v, sink):
    B, S, HQ, dh = q.shape
    nb = S // SWA_BLOCK
    qb = q.reshape(B, nb, SWA_BLOCK, SWA_KV_HEADS, SWA_GROUP, dh)
    pad = ((0, 0), (SWA_BLOCK, SWA_BLOCK), (0, 0), (0, 0))
    kb = jnp.pad(k, pad).reshape(B, nb + 2, SWA_BLOCK, SWA_KV_HEADS, dh)
    vb = jnp.pad(v, pad).reshape(B, nb + 2, SWA_BLOCK, SWA_KV_HEADS, dh)
    kwin = jnp.concatenate([kb[:, :-2], kb[:, 1:-1], kb[:, 2:]], axis=2)
    vwin = jnp.concatenate([vb[:, :-2], vb[:, 1:-1], vb[:, 2:]], axis=2)
    s = jnp.einsum('bnqkgd,bnjkd->bnkgqj', qb, kwin).astype(jnp.float32) * (dh ** -0.5)
    qi = jnp.arange(SWA_BLOCK)
    kj = jnp.arange(3 * SWA_BLOCK)
    rel = kj[None, :] - SWA_BLOCK - qi[:, None]
    kabs = jnp.arange(nb)[:, None] * SWA_BLOCK - SWA_BLOCK + kj[None, :]
    valid = (jnp.abs(rel) <= SWA_WINDOW)[None] & ((kabs >= 0) & (kabs < S))[:, None, :]
    slopes = alibi_slopes(HQ).reshape(SWA_KV_HEADS, SWA_GROUP)
    s = s - slopes[None, None, :, :, None, None] * jnp.abs(rel).astype(jnp.float32)[None, None, None, None]
    s = jnp.where(valid[None, :, None, None], s, NEG_INF)
    sk = sink.astype(jnp.float32).reshape(SWA_KV_HEADS, SWA_GROUP)[None, None, :, :, None]
    m = jnp.maximum(s.max(axis=-1), sk)
    e = jnp.exp(s - m[..., None])
    p = e / (e.sum(axis=-1) + jnp.exp(sk - m))[..., None]
    o = jnp.einsum('bnkgqj,bnjkd->bnqkgd', p.astype(v.dtype), vwin)
    return o.reshape(B, S, HQ, dh)


def memory_cross_attention(h, m, wq, wkv, wo):
    B, S, _ = h.shape
    M = m.shape[1]
    q = (h @ wq).reshape(B, S, MEM_HEADS, HEAD_DIM)
    kv = (m @ wkv).reshape(B, M, 2, MEM_HEADS, HEAD_DIM)
    k, v = kv[:, :, 0], kv[:, :, 1]
    s = jnp.einsum('bshd,bmhd->bhsm', q, k).astype(jnp.float32) * (HEAD_DIM ** -0.5)
    p = jax.nn.softmax(s, axis=-1)
    o = jnp.einsum('bhsm,bmhd->bshd', p.astype(v.dtype), v).reshape(B, S, MEM_WIDTH)
    return o @ wo


def expert_choice_moe(h, w_router, w_gate, w_up, w_down):
    B, S, D = h.shape
    n = B * S
    cap = EC_CAPACITY_FACTOR * n // N_EXPERTS
    tok = h.reshape(n, D)
    aff = jax.nn.softmax((tok @ w_router).astype(jnp.float32), axis=-1)
    gates, idx = lax.top_k(aff.T, cap)
    xe = tok[idx]
    a = jnp.einsum('ecd,edf->ecf', xe, w_gate)
    b = jnp.einsum('ecd,edf->ecf', xe, w_up)
    y = jnp.einsum('ecf,efd->ecd', jax.nn.silu(a) * b, w_down) * gates[..., None].astype(h.dtype)
    out = jnp.zeros_like(tok).at[idx.reshape(-1)].add(y.reshape(-1, D))
    return out.reshape(B, S, D)


def encoder_layer(x, mem, w_in, w_out, na_bias, fnet_w, sink_logit, wq_mem, wkv_mem, wo_mem,
                  w_router, w_gate, w_up, w_down, g_mix, g_mem_q, g_mem_kv, g_ffn):
    B, S, _ = x.shape
    h = rmsnorm(x, g_mix)
    z = h @ w_in
    cuts = np.cumsum([NA_WIDTH, NA_WIDTH, NA_WIDTH, FN_WIDTH, SWA_WIDTH, SWA_KV_WIDTH]).tolist()
    qa, ka, va, ub, qc, kc, vc = jnp.split(z, cuts, axis=-1)
    hs = lambda t, nh: t.reshape(B, S, nh, HEAD_DIM)
    a = neighborhood_attention(hs(qa, NA_HEADS), hs(ka, NA_HEADS), hs(va, NA_HEADS), na_bias)
    f = fourier_mix(ub.reshape(B, S, FN_GROUPS, FN_GROUP_DIM), fnet_w)
    c = window_gqa(hs(qc, SWA_Q_HEADS), hs(kc, SWA_KV_HEADS), hs(vc, SWA_KV_HEADS), sink_logit)
    mixed = jnp.concatenate([a.reshape(B, S, NA_WIDTH), f.reshape(B, S, FN_WIDTH),
                             c.reshape(B, S, SWA_WIDTH)], axis=-1)
    x = x + mixed @ w_out
    x = x + memory_cross_attention(rmsnorm(x, g_mem_q), rmsnorm(mem, g_mem_kv), wq_mem, wkv_mem, wo_mem)
    x = x + expert_choice_moe(rmsnorm(x, g_ffn), w_router, w_gate, w_up, w_down)
    return x


def setup_inputs(seed: int = 0) -> dict:
    key = jax.random.key(seed)
    ks = jax.random.split(key, 24)

    def nrm(k, shape, scale):
        return jax.random.normal(k, shape, jnp.float32) * scale

    def gain(k, shape):
        return 1.0 + 0.05 * jax.random.normal(k, shape, jnp.float32)

    return {
        "x_prompt": nrm(ks[0], (BATCH, SEQ, D_MODEL), 1.0),
        "x_sample": nrm(ks[1], (DEC_BATCH, DEC_SEQ, D_MODEL), 1.0),
        "mem_prompt": nrm(ks[2], (BATCH, MEM_TOKENS, D_MODEL), 1.0),
        "mem_sample": nrm(ks[3], (DEC_BATCH, MEM_TOKENS, D_MODEL), 1.0),
        "w_in": nrm(ks[4], (DEPTH, D_MODEL, IN_WIDTH), D_MODEL ** -0.5),
        "w_out": nrm(ks[5], (DEPTH, MIX_WIDTH, D_MODEL), MIX_WIDTH ** -0.5),
        "na_bias": nrm(ks[6], (DEPTH, NA_HEADS, 2 * NA_ROWS - 1, 2 * NA_COLS - 1), 0.1),
        "fnet_w": nrm(ks[7], (DEPTH, FN_GROUPS, FN_GROUP_DIM, FN_GROUP_DIM), FN_GROUP_DIM ** -0.5),
        "sink_logit": nrm(ks[8], (DEPTH, SWA_Q_HEADS), 0.5),
        "wq_mem": nrm(ks[9], (DEPTH, D_MODEL, MEM_WIDTH), D_MODEL ** -0.5),
        "wkv_mem": nrm(ks[10], (DEPTH, D_MODEL, 2 * MEM_WIDTH), D_MODEL ** -0.5),
        "wo_mem": nrm(ks[11], (DEPTH, MEM_WIDTH, D_MODEL), MEM_WIDTH ** -0.5),
        "w_router": nrm(ks[12], (DEPTH, D_MODEL, N_EXPERTS), D_MODEL ** -0.5),
        "w_gate": nrm(ks[13], (DEPTH, N_EXPERTS, D_MODEL, D_FF_EXPERT), D_MODEL ** -0.5),
        "w_up": nrm(ks[14], (DEPTH, N_EXPERTS, D_MODEL, D_FF_EXPERT), D_MODEL ** -0.5),
        "w_down": nrm(ks[15], (DEPTH, N_EXPERTS, D_FF_EXPERT, D_MODEL), D_FF_EXPERT ** -0.5),
        "g_mix": gain(ks[16], (DEPTH, D_MODEL)),
        "g_mem_q": gain(ks[17], (DEPTH, D_MODEL)),
        "g_mem_kv": gain(ks[18], (DEPTH, D_MODEL)),
        "g_ffn": gain(ks[19], (DEPTH, D_MODEL)),
        "g_final": gain(ks[20], (D_MODEL,)),
    }


def reference(x_prompt, x_sample, mem_prompt, mem_sample, w_in, w_out, na_bias, fnet_w, sink_logit,
              wq_mem, wkv_mem, wo_mem, w_router, w_gate, w_up, w_down,
              g_mix, g_mem_q, g_mem_kv, g_ffn, g_final):
    xp = x_prompt
    xs = x_sample
    for l in range(DEPTH):
        params = (w_in[l], w_out[l], na_bias[l], fnet_w[l], sink_logit[l], wq_mem[l], wkv_mem[l],
                  wo_mem[l], w_router[l], w_gate[l], w_up[l], w_down[l],
                  g_mix[l], g_mem_q[l], g_mem_kv[l], g_ffn[l])
        xp = encoder_layer(xp, mem_prompt, *params)
        xs = encoder_layer(xs, mem_sample, *params)
    y_prompt = rmsnorm(xp, g_final)
    y_sample = rmsnorm(xs, g_final)
    return (y_prompt, y_sample)
```

```python
import functools
import math

import jax
import jax.numpy as jnp
import numpy as np
from jax import lax
from jax.experimental import pallas as pl
from jax.experimental.pallas import tpu as pltpu

HEAD_DIM = 128
NA_HEADS = 4
NA_WIDTH = NA_HEADS * HEAD_DIM
FN_GROUPS = 4
FN_GROUP_DIM = 128
FN_WIDTH = FN_GROUPS * FN_GROUP_DIM
SWA_Q_HEADS = 8
SWA_KV_HEADS = 2
SWA_GROUP = SWA_Q_HEADS // SWA_KV_HEADS
SWA_WIDTH = SWA_Q_HEADS * HEAD_DIM
SWA_KV_WIDTH = SWA_KV_HEADS * HEAD_DIM
GRID_W = 64
NA_ROWS = 8
NA_COLS = 16
SWA_WINDOW = 128
SWA_BLOCK = 128
MEM_HEADS = 4
MEM_WIDTH = MEM_HEADS * HEAD_DIM
EC_CAPACITY_FACTOR = 2
RMS_EPS = 1e-6
NEG_INF = -1e30

V7X_VMEM_LIMIT_BYTES = 56 * 1024 * 1024

BF16 = jnp.bfloat16
F32 = jnp.float32


def _params(*semantics):
    return pltpu.CompilerParams(dimension_semantics=semantics, vmem_limit_bytes=V7X_VMEM_LIMIT_BYTES)


def _rms(x, g):
    ms = jnp.mean(x * x, axis=-1, keepdims=True)
    return x * lax.rsqrt(ms + RMS_EPS) * g


def _row_tile(rows, want):
    t = min(rows, want)
    while rows % t:
        t //= 2
    return t


def _norm_matmul_kernel(x_ref, g_ref, w_ref, o_ref):
    h = _rms(x_ref[...], g_ref[...]).astype(BF16)
    o_ref[...] = jnp.dot(h, w_ref[...], preferred_element_type=F32).astype(o_ref.dtype)


def norm_matmul(x, g, w, out_dtype, tm=512):
    rows, d = x.shape
    n = w.shape[1]
    tm = _row_tile(rows, tm)
    return pl.pallas_call(
        _norm_matmul_kernel,
        grid=(rows // tm,),
        in_specs=[
            pl.BlockSpec((tm, d), lambda i: (i, 0)),
            pl.BlockSpec((1, d), lambda i: (0, 0)),
            pl.BlockSpec((d, n), lambda i: (0, 0)),
        ],
        out_specs=pl.BlockSpec((tm, n), lambda i: (i, 0)),
        out_shape=jax.ShapeDtypeStruct((rows, n), out_dtype),
        compiler_params=_params("parallel"),
        name="norm_matmul",
    )(x, g.reshape(1, d), w)


def _post_mix_kernel(x_ref, mix_ref, wout_ref, gq_ref, wq_ref, kv_ref, wo_ref, gf_ref, wr_ref,
                     x2_ref, h3_ref, aff_ref):
    x1 = x_ref[...] + jnp.dot(mix_ref[...], wout_ref[...], preferred_element_type=F32)
    hq = _rms(x1, gq_ref[...]).astype(BF16)
    q = jnp.dot(hq, wq_ref[...], preferred_element_type=F32).astype(BF16)
    kv = kv_ref[...]
    scale = HEAD_DIM ** -0.5
    heads = []
    for h in range(MEM_HEADS):
        qh = q[:, h * HEAD_DIM:(h + 1) * HEAD_DIM]
        kh = kv[:, h * HEAD_DIM:(h + 1) * HEAD_DIM]
        vh = kv[:, MEM_WIDTH + h * HEAD_DIM:MEM_WIDTH + (h + 1) * HEAD_DIM]
        s = lax.dot_general(qh, kh, (((1,), (1,)), ((), ())), preferred_element_type=F32) * scale
        e = jnp.exp(s - jnp.max(s, axis=-1, keepdims=True))
        p = e / jnp.sum(e, axis=-1, keepdims=True)
        heads.append(jnp.dot(p.astype(BF16), vh, preferred_element_type=F32).astype(BF16))
    o = jnp.concatenate(heads, axis=-1)
    x2 = x1 + jnp.dot(o, wo_ref[...], preferred_element_type=F32)
    x2_ref[...] = x2
    h3 = _rms(x2, gf_ref[...]).astype(BF16)
    h3_ref[...] = h3
    logits = jnp.dot(h3, wr_ref[...], preferred_element_type=F32)
    el = jnp.exp(logits - jnp.max(logits, axis=-1, keepdims=True))
    aff_ref[...] = el / jnp.sum(el, axis=-1, keepdims=True)


def post_mix(x, mixed, w_out, g_q, wq, kv, wo, g_f, w_router, seq_lens, tm=512):
    rows, d = x.shape
    n_mem = kv.shape[0] // len(seq_lens)
    n_exp = w_router.shape[1]
    tm = _row_tile(math.gcd(*seq_lens), tm)
    bounds = np.cumsum(seq_lens)[:-1] // tm

    def seq_of(i):
        s = jnp.int32(0)
        for b in bounds:
            s = s + (i >= int(b)).astype(jnp.int32)
        return s

    const = lambda i: (0, 0)
    return pl.pallas_call(
        _post_mix_kernel,
        grid=(rows // tm,),
        in_specs=[
            pl.BlockSpec((tm, d), lambda i: (i, 0)),
            pl.BlockSpec((tm, mixed.shape[1]), lambda i: (i, 0)),
            pl.BlockSpec(w_out.shape, const),
            pl.BlockSpec((1, d), const),
            pl.BlockSpec(wq.shape, const),
            pl.BlockSpec((n_mem, kv.shape[1]), lambda i: (seq_of(i), 0)),
            pl.BlockSpec(wo.shape, const),
            pl.BlockSpec((1, d), const),
            pl.BlockSpec(w_router.shape, const),
        ],
        out_specs=[
            pl.BlockSpec((tm, d), lambda i: (i, 0)),
            pl.BlockSpec((tm, d), lambda i: (i, 0)),
            pl.BlockSpec((tm, n_exp), lambda i: (i, 0)),
        ],
        out_shape=[
            jax.ShapeDtypeStruct((rows, d), F32),
            jax.ShapeDtypeStruct((rows, d), BF16),
            jax.ShapeDtypeStruct((rows, n_exp), F32),
        ],
        compiler_params=_params("parallel"),
        name="post_mix",
    )(x, mixed, w_out, g_q.reshape(1, d), wq, kv, wo, g_f.reshape(1, d), w_router)


def _expert_ffn_kernel(xe_ref, gate_ref, wg_ref, wu_ref, wd_ref, y_ref):
    f = pl.program_id(2)
    xe = xe_ref[0]
    a = jnp.dot(xe, wg_ref[0].astype(BF16), preferred_element_type=F32)
    b = jnp.dot(xe, wu_ref[0].astype(BF16), preferred_element_type=F32)
    hid = (a * jax.nn.sigmoid(a) * b).astype(BF16)
    part = jnp.dot(hid, wd_ref[0].astype(BF16), preferred_element_type=F32)

    @pl.when(f == 0)
    def _():
        y_ref[0] = part

    @pl.when(f > 0)
    def _():
        y_ref[0] += part

    @pl.when(f == pl.num_programs(2) - 1)
    def _():
        y_ref[0] = y_ref[0] * gate_ref[0]


def expert_ffn(xe, gates, w_gate, w_up, w_down, tm=1024, fk=256):
    n_exp, rows, d = xe.shape
    d_ff = w_gate.shape[2]
    tm = _row_tile(rows, tm)
    fk = _row_tile(d_ff, fk)
    return pl.pallas_call(
        _expert_ffn_kernel,
        grid=(n_exp, rows // tm, d_ff // fk),
        in_specs=[
            pl.BlockSpec((1, tm, d), lambda e, m, f: (e, m, 0)),
            pl.BlockSpec((1, tm, 1), lambda e, m, f: (e, m, 0)),
            pl.BlockSpec((1, d, fk), lambda e, m, f: (e, 0, f)),
            pl.BlockSpec((1, d, fk), lambda e, m, f: (e, 0, f)),
            pl.BlockSpec((1, fk, d), lambda e, m, f: (e, f, 0)),
        ],
        out_specs=pl.BlockSpec((1, tm, d), lambda e, m, f: (e, m, 0)),
        out_shape=jax.ShapeDtypeStruct((n_exp, rows, d), F32),
        compiler_params=_params("parallel", "parallel", "arbitrary"),
        name="expert_ffn",
    )(xe, gates, w_gate, w_up, w_down)


def _final_norm_kernel(x_ref, g_ref, o_ref):
    o_ref[...] = _rms(x_ref[...], g_ref[...])


def final_norm(x, g, tm=1024):
    rows, d = x.shape
    tm = _row_tile(rows, tm)
    return pl.pallas_call(
        _final_norm_kernel,
        grid=(rows // tm,),
        in_specs=[pl.BlockSpec((tm, d), lambda i: (i, 0)), pl.BlockSpec((1, d), lambda i: (0, 0))],
        out_specs=pl.BlockSpec((tm, d), lambda i: (i, 0)),
        out_shape=jax.ShapeDtypeStruct((rows, d), F32),
        compiler_params=_params("parallel"),
        name="final_norm",
    )(x, g.reshape(1, d))


def _alibi_slopes(n_heads):
    return jnp.exp2(-jnp.arange(1, n_heads + 1, dtype=F32) * (8.0 / n_heads))


def _neighborhood_attention(q, k, v, bias_tab):
    B, S, H, dh = q.shape
    rows = S // GRID_W
    wr = min(NA_ROWS, rows)
    qg = q.reshape(B, rows, GRID_W, H, dh)
    kg = k.reshape(B, rows, GRID_W, H, dh)
    vg = v.reshape(B, rows, GRID_W, H, dh)
    r = jnp.arange(rows)
    r0 = jnp.clip(r - wr // 2, 0, rows - wr)
    krow = r0[:, None] + jnp.arange(wr)[None, :]
    kn = kg[:, krow]
    vn = vg[:, krow]
    s = jnp.einsum('brqhd,brwkhd->brhqwk', qg, kn) * (dh ** -0.5)
    c = jnp.arange(GRID_W)
    c0 = jnp.clip(c - NA_COLS // 2, 0, GRID_W - NA_COLS)
    in_col = (c[None, :] >= c0[:, None]) & (c[None, :] < c0[:, None] + NA_COLS)
    dc_idx = jnp.clip(c[None, :] - c[:, None], -(NA_COLS - 1), NA_COLS - 1) + NA_COLS - 1
    dr_idx = krow - r[:, None] + NA_ROWS - 1
    bias = bias_tab.astype(F32)[:, dr_idx]
    bias = bias[..., dc_idx]
    bias = bias.transpose(1, 0, 3, 2, 4)
    s = jnp.where(in_col[None, None, None, :, None, :], s + bias[None], NEG_INF)
    p = jax.nn.softmax(s, axis=(-2, -1))
    o = jnp.einsum('brhqwk,brwkhd->brqhd', p, vn)
    return o.reshape(B, S, H * dh)


def _fourier_mix(u, w):
    f = jnp.fft.fft2(u.astype(F32), axes=(1, 3), norm='ortho').real
    return jnp.einsum('bsgc,gcd->bsgd', f, w)


def _window_gqa(q, k, v, sink):
    B, S, HQ, dh = q.shape
    nb = S // SWA_BLOCK
    qb = q.reshape(B, nb, SWA_BLOCK, SWA_KV_HEADS, SWA_GROUP, dh)
    pad = ((0, 0), (SWA_BLOCK, SWA_BLOCK), (0, 0), (0, 0))
    kb = jnp.pad(k, pad).reshape(B, nb + 2, SWA_BLOCK, SWA_KV_HEADS, dh)
    vb = jnp.pad(v, pad).reshape(B, nb + 2, SWA_BLOCK, SWA_KV_HEADS, dh)
    kwin = jnp.concatenate([kb[:, :-2], kb[:, 1:-1], kb[:, 2:]], axis=2)
    vwin = jnp.concatenate([vb[:, :-2], vb[:, 1:-1], vb[:, 2:]], axis=2)
    s = jnp.einsum('bnqkgd,bnjkd->bnkgqj', qb, kwin) * (dh ** -0.5)
    qi = jnp.arange(SWA_BLOCK)
    kj = jnp.arange(3 * SWA_BLOCK)
    rel = kj[None, :] - SWA_BLOCK - qi[:, None]
    kabs = jnp.arange(nb)[:, None] * SWA_BLOCK - SWA_BLOCK + kj[None, :]
    valid = (jnp.abs(rel) <= SWA_WINDOW)[None] & ((kabs >= 0) & (kabs < S))[:, None, :]
    slopes = _alibi_slopes(HQ).reshape(SWA_KV_HEADS, SWA_GROUP)
    s = s - slopes[None, None, :, :, None, None] * jnp.abs(rel).astype(F32)[None, None, None, None]
    s = jnp.where(valid[None, :, None, None], s, NEG_INF)
    sk = sink.astype(F32).reshape(SWA_KV_HEADS, SWA_GROUP)[None, None, :, :, None]
    m = jnp.maximum(s.max(axis=-1), sk)
    e = jnp.exp(s - m[..., None])
    p = e / (e.sum(axis=-1) + jnp.exp(sk - m))[..., None]
    o = jnp.einsum('bnkgqj,bnjkd->bnqkgd', p, vwin)
    return o.reshape(B, S, HQ * dh)


def _mixers(z, na_bias, fnet_w, sink):
    B, S, _ = z.shape
    z = z.astype(F32)
    cuts = np.cumsum([NA_WIDTH, NA_WIDTH, NA_WIDTH, FN_WIDTH, SWA_WIDTH, SWA_KV_WIDTH]).tolist()
    qa, ka, va, ub, qc, kc, vc = jnp.split(z, cuts, axis=-1)
    hs = lambda t, nh: t.reshape(B, S, nh, HEAD_DIM)
    a = _neighborhood_attention(hs(qa, NA_HEADS), hs(ka, NA_HEADS), hs(va, NA_HEADS), na_bias)
    f = _fourier_mix(ub.reshape(B, S, FN_GROUPS, FN_GROUP_DIM), fnet_w).reshape(B, S, FN_WIDTH)
    c = _window_gqa(hs(qc, SWA_Q_HEADS), hs(kc, SWA_KV_HEADS), hs(vc, SWA_KV_HEADS), sink)
    return jnp.concatenate([a, f, c], axis=-1).astype(BF16)


def kernel(x_prompt, x_sample, mem_prompt, mem_sample, w_in, w_out, na_bias, fnet_w, sink_logit,
           wq_mem, wkv_mem, wo_mem, w_router, w_gate, w_up, w_down,
           g_mix, g_mem_q, g_mem_kv, g_ffn, g_final):
    d = x_prompt.shape[-1]
    depth = w_in.shape[0]
    n_exp = w_router.shape[-1]
    groups = [(x_prompt.shape[0], x_prompt.shape[1]), (x_sample.shape[0], x_sample.shape[1])]
    seq_lens = [s for b, s in groups for _ in range(b)]
    group_tokens = [b * s for b, s in groups]
    caps = [EC_CAPACITY_FACTOR * n // n_exp for n in group_tokens]
    offs = np.concatenate([[0], np.cumsum(group_tokens)]).tolist()

    x = jnp.concatenate([x_prompt.reshape(-1, d), x_sample.reshape(-1, d)], axis=0)
    mem = jnp.concatenate([mem_prompt.reshape(-1, d), mem_sample.reshape(-1, d)], axis=0)

    w_in_b, w_out_b = w_in.astype(BF16), w_out.astype(BF16)
    wq_b, wkv_b, wo_b = wq_mem.astype(BF16), wkv_mem.astype(BF16), wo_mem.astype(BF16)
    wr_b, fnet_b = w_router.astype(BF16), fnet_w.astype(BF16)

    for l in range(depth):
        z = norm_matmul(x, g_mix[l], w_in_b[l], BF16)
        mixed = []
        for gi, (b, s) in enumerate(groups):
            zg = z[offs[gi]:offs[gi + 1]].reshape(b, s, -1)
            mixed.append(_mixers(zg, na_bias[l], fnet_b[l], sink_logit[l]).reshape(b * s, -1))
        mixed = jnp.concatenate(mixed, axis=0)
        kv = norm_matmul(mem, g_mem_kv[l], wkv_b[l], BF16)
        x2, h3, aff = post_mix(x, mixed, w_out_b[l], g_mem_q[l], wq_b[l], kv, wo_b[l], g_ffn[l],
                               wr_b[l], seq_lens)
        xes, gts, idxs = [], [], []
        for gi in range(len(groups)):
            ag = aff[offs[gi]:offs[gi + 1]]
            g_, i_ = lax.top_k(ag.T, caps[gi])
            idxs.append(i_)
            gts.append(g_)
            xes.append(h3[offs[gi]:offs[gi + 1]][i_])
        y = expert_ffn(jnp.concatenate(xes, axis=1), jnp.concatenate(gts, axis=1)[..., None],
                       w_gate[l], w_up[l], w_down[l])
        outs = []
        c0 = 0
        for gi in range(len(groups)):
            yg = y[:, c0:c0 + caps[gi]].reshape(-1, d)
            c0 += caps[gi]
            xg = x2[offs[gi]:offs[gi + 1]]
            outs.append(xg.at[idxs[gi].reshape(-1)].add(yg))
        x = jnp.concatenate(outs, axis=0)

    y = final_norm(x, g_final)
    y_prompt = y[:offs[1]].reshape(x_prompt.shape)
    y_sample = y[offs[1]:].reshape(x_sample.shape)
    return (y_prompt, y_sample)
```

```python
import functools
import math

import jax
import jax.numpy as jnp
import numpy as np
from jax import lax
from jax.experimental import pallas as pl
from jax.experimental.pallas import tpu as pltpu

HEAD_DIM = 128
NA_HEADS = 4
NA_WIDTH = NA_HEADS * HEAD_DIM
FN_GROUPS = 4
FN_GROUP_DIM = 128
FN_WIDTH = FN_GROUPS * FN_GROUP_DIM
SWA_Q_HEADS = 8
SWA_KV_HEADS = 2
SWA_GROUP = SWA_Q_HEADS // SWA_KV_HEADS
SWA_WIDTH = SWA_Q_HEADS * HEAD_DIM
SWA_KV_WIDTH = SWA_KV_HEADS * HEAD_DIM
GRID_W = 64
NA_ROWS = 8
NA_COLS = 16
SWA_WINDOW = 128
SWA_BLOCK = 128
MEM_HEADS = 4
MEM_WIDTH = MEM_HEADS * HEAD_DIM
EC_CAPACITY_FACTOR = 2
RMS_EPS = 1e-6
NEG_INF = -1e30

Z_NA_Q, Z_NA_K, Z_NA_V = 0, NA_WIDTH, 2 * NA_WIDTH
Z_FN = 3 * NA_WIDTH
Z_SWA_Q = Z_FN + FN_WIDTH
Z_SWA_K = Z_SWA_Q + SWA_WIDTH
Z_SWA_V = Z_SWA_K + SWA_KV_WIDTH

V7X_VMEM_LIMIT_BYTES = 56 * 1024 * 1024

BF16 = jnp.bfloat16
F32 = jnp.float32
HIGHEST = lax.Precision.HIGHEST


def _params(*semantics):
    return pltpu.CompilerParams(dimension_semantics=semantics, vmem_limit_bytes=V7X_VMEM_LIMIT_BYTES)


def _rms(x, g):
    ms = jnp.mean(x * x, axis=-1, keepdims=True)
    return x * lax.rsqrt(ms + RMS_EPS) * g


def _row_tile(rows, want):
    t = min(rows, want)
    while rows % t:
        t //= 2
    return t


def _edge_fns(seq_lens, blk):
    firsts, lasts = [], []
    o = 0
    for s in seq_lens:
        firsts.append(o // blk)
        o += s
        lasts.append(o // blk - 1)

    def first_of(i):
        f = jnp.int32(firsts[0])
        for b in firsts[1:]:
            f = jnp.where(i >= b, jnp.int32(b), f)
        return f

    def last_of(i):
        l = jnp.int32(lasts[-1])
        for b in reversed(lasts[:-1]):
            l = jnp.where(i <= b, jnp.int32(b), l)
        return l

    return first_of, last_of


def _norm_matmul_kernel(x_ref, g_ref, w_ref, o_ref):
    h = _rms(x_ref[...], g_ref[...]).astype(BF16)
    o_ref[...] = jnp.dot(h, w_ref[...], preferred_element_type=F32).astype(o_ref.dtype)


def norm_matmul(x, g, w, out_dtype, tm=512):
    rows, d = x.shape
    n = w.shape[1]
    tm = _row_tile(rows, tm)
    return pl.pallas_call(
        _norm_matmul_kernel,
        grid=(rows // tm,),
        in_specs=[
            pl.BlockSpec((tm, d), lambda i: (i, 0)),
            pl.BlockSpec((1, d), lambda i: (0, 0)),
            pl.BlockSpec((d, n), lambda i: (0, 0)),
        ],
        out_specs=pl.BlockSpec((tm, n), lambda i: (i, 0)),
        out_shape=jax.ShapeDtypeStruct((rows, n), out_dtype),
        compiler_params=_params("parallel"),
        name="norm_matmul",
    )(x, g.reshape(1, d), w)


def _swa_kernel(first_of, last_of, q_ref, kp_ref, kc_ref, kn_ref, vp_ref, vc_ref, vn_ref, pen_ref,
                sink_ref, o_ref):
    i = pl.program_id(0)
    is_first = i == first_of(i)
    is_last = i == last_of(i)
    q = q_ref[...]
    kp, kc, kn = kp_ref[...], kc_ref[...], kn_ref[...]
    vp, vc, vn = vp_ref[...], vc_ref[...], vn_ref[...]
    scale = HEAD_DIM ** -0.5
    blk = SWA_BLOCK
    outs = []
    for g in range(SWA_KV_HEADS):
        sl = slice(g * HEAD_DIM, (g + 1) * HEAD_DIM)
        qg = jnp.concatenate([q[:, (g * SWA_GROUP + j) * HEAD_DIM:(g * SWA_GROUP + j + 1) * HEAD_DIM]
                              for j in range(SWA_GROUP)], axis=0)
        kw = jnp.concatenate([kp[:, sl], kc[:, sl], kn[:, sl]], axis=0)
        vw = jnp.concatenate([vp[:, sl], vc[:, sl], vn[:, sl]], axis=0)
        s = lax.dot_general(qg, kw, (((1,), (1,)), ((), ())), preferred_element_type=F32)
        s = s * scale - pen_ref[g]
        col = lax.broadcasted_iota(jnp.int32, s.shape, 1)
        outside = (is_first & (col < blk)) | (is_last & (col >= 2 * blk))
        s = jnp.where(outside, NEG_INF, s)
        sk = sink_ref[g]
        m = jnp.maximum(jnp.max(s, axis=-1, keepdims=True), sk)
        e = jnp.exp(s - m)
        den = jnp.sum(e, axis=-1, keepdims=True) + jnp.exp(sk - m)
        p = (e / den).astype(BF16)
        og = jnp.dot(p, vw, preferred_element_type=F32).astype(BF16)
        outs.extend([og[j * blk:(j + 1) * blk] for j in range(SWA_GROUP)])
    o_ref[...] = jnp.concatenate(outs, axis=-1)


def swa_attention(z, sink_logit, seq_lens):
    t = z.shape[0]
    blk = SWA_BLOCK
    assert all(s % blk == 0 for s in seq_lens)
    first_of, last_of = _edge_fns(seq_lens, blk)
    qcol, kcol, vcol = Z_SWA_Q // SWA_WIDTH, Z_SWA_K // SWA_KV_WIDTH, Z_SWA_V // SWA_KV_WIDTH
    assert qcol * SWA_WIDTH == Z_SWA_Q and kcol * SWA_KV_WIDTH == Z_SWA_K
    qi = np.arange(blk)[:, None]
    kj = np.arange(3 * blk)[None, :]
    rel = np.abs(kj - blk - qi).astype(np.float32)
    slopes = jnp.exp2(-jnp.arange(1, SWA_Q_HEADS + 1, dtype=F32) * (8.0 / SWA_Q_HEADS))
    pen = jnp.where(jnp.asarray(rel <= SWA_WINDOW)[None], slopes[:, None, None] * rel[None], -NEG_INF)
    pen = pen.reshape(SWA_KV_HEADS, SWA_GROUP * blk, 3 * blk)
    sink = jnp.repeat(sink_logit.astype(F32), blk).reshape(SWA_KV_HEADS, SWA_GROUP * blk, 1)
    prev = lambda i: jnp.maximum(i - 1, first_of(i))
    nxt = lambda i: jnp.minimum(i + 1, last_of(i))
    same = lambda i: i
    kv = lambda col, f: pl.BlockSpec((blk, SWA_KV_WIDTH), lambda i: (f(i), col))
    return pl.pallas_call(
        functools.partial(_swa_kernel, first_of, last_of),
        grid=(t // blk,),
        in_specs=[
            pl.BlockSpec((blk, SWA_WIDTH), lambda i: (i, qcol)),
            kv(kcol, prev), kv(kcol, same), kv(kcol, nxt),
            kv(vcol, prev), kv(vcol, same), kv(vcol, nxt),
            pl.BlockSpec(pen.shape, lambda i: (0, 0, 0)),
            pl.BlockSpec(sink.shape, lambda i: (0, 0, 0)),
        ],
        out_specs=pl.BlockSpec((blk, SWA_WIDTH), lambda i: (i, 0)),
        out_shape=jax.ShapeDtypeStruct((t, SWA_WIDTH), BF16),
        compiler_params=_params("parallel"),
        name="swa_attention",
    )(z, z, z, z, z, z, z, pen, sink)


NA_QROWS = 4
NA_QBLK = NA_QROWS * GRID_W
NA_PAIRS = 3 * NA_QROWS // 2


def _na_kernel(first_of, last_of, q_ref, kp_ref, kc_ref, kn_ref, vp_ref, vc_ref, vn_ref, bias_ref,
               rowmask_ref, o_ref):
    b = pl.program_id(0)
    first = first_of(b)
    n_rows = NA_QROWS * (last_of(b) - first + 1)
    q = q_ref[...]
    kp, kc, kn = kp_ref[...], kc_ref[...], kn_ref[...]
    vp, vc, vn = vp_ref[...], vc_ref[...], vn_ref[...]
    scale = HEAD_DIM ** -0.5
    codes = []
    for i in range(NA_QROWS):
        r = NA_QROWS * (b - first) + i
        r0 = jnp.clip(r - NA_ROWS // 2, 0, n_rows - NA_ROWS)
        v = r - r0
        in_window = lambda dr: ((dr >= NA_ROWS - 1 - v) & (dr < 2 * NA_ROWS - 1 - v)).astype(jnp.int32)
        codes.append([in_window(2 * p - i + NA_QROWS - 1) + 2 * in_window(2 * p - i + NA_QROWS)
                      for p in range(NA_PAIRS)])
    outs = []
    for h in range(NA_HEADS):
        sl = slice(h * HEAD_DIM, (h + 1) * HEAD_DIM)
        kw = jnp.concatenate([kp[:, sl], kc[:, sl], kn[:, sl]], axis=0)
        vw = jnp.concatenate([vp[:, sl], vc[:, sl], vn[:, sl]], axis=0)
        s = lax.dot_general(q[:, sl], kw, (((1,), (1,)), ((), ())), preferred_element_type=F32) * scale
        probs = []
        for i in range(NA_QROWS):
            tiles = []
            for p in range(NA_PAIRS):
                dr0 = 2 * p - i + NA_QROWS - 1
                t = s[i * GRID_W:(i + 1) * GRID_W, p * 2 * GRID_W:(p + 1) * 2 * GRID_W]
                tiles.append(t + bias_ref[h, dr0] + rowmask_ref[codes[i][p]])
            si = jnp.concatenate(tiles, axis=-1)
            e = jnp.exp(si - jnp.max(si, axis=-1, keepdims=True))
            probs.append((e / jnp.sum(e, axis=-1, keepdims=True)).astype(BF16))
        pm = jnp.concatenate(probs, axis=0)
        outs.append(jnp.dot(pm, vw, preferred_element_type=F32).astype(BF16))
    o_ref[...] = jnp.concatenate(outs, axis=-1)


def _na_tables(na_bias):
    c = np.arange(GRID_W)
    c0 = np.clip(c - NA_COLS // 2, 0, GRID_W - NA_COLS)
    in_col = (c[None, :] >= c0[:, None]) & (c[None, :] < c0[:, None] + NA_COLS)
    dc_idx = np.clip(c[None, :] - c[:, None], -(NA_COLS - 1), NA_COLS - 1) + NA_COLS - 1
    tab = na_bias.astype(F32)[:, :, dc_idx]
    tab = jnp.where(jnp.asarray(in_col)[None, None], tab, NEG_INF)
    pair = jnp.concatenate([tab[:, :-1], tab[:, 1:]], axis=-1)
    left = np.arange(2 * GRID_W) < GRID_W
    masks = [np.broadcast_to(np.where(np.where(left, code & 1, code >> 1) > 0, 0.0, NEG_INF), (GRID_W, 2 * GRID_W))
             for code in range(4)]
    return pair, jnp.asarray(np.stack(masks), F32)


def na_attention(z, na_bias, seq_lens):
    t = z.shape[0]
    assert all(s % NA_QBLK == 0 and s // GRID_W >= 2 * NA_ROWS for s in seq_lens)
    assert 2 * NA_QROWS == NA_ROWS
    first_of, last_of = _edge_fns(seq_lens, NA_QBLK)
    pair, rowmask = _na_tables(na_bias)
    prev = lambda i: jnp.maximum(i - 1, first_of(i))
    nxt = lambda i: jnp.minimum(i + 1, last_of(i))
    same = lambda i: i
    blk = lambda col, f: pl.BlockSpec((NA_QBLK, NA_WIDTH), lambda i: (f(i), col))
    qcol, kcol, vcol = Z_NA_Q // NA_WIDTH, Z_NA_K // NA_WIDTH, Z_NA_V // NA_WIDTH
    return pl.pallas_call(
        functools.partial(_na_kernel, first_of, last_of),
        grid=(t // NA_QBLK,),
        in_specs=[
            blk(qcol, same),
            blk(kcol, prev), blk(kcol, same), blk(kcol, nxt),
            blk(vcol, prev), blk(vcol, same), blk(vcol, nxt),
            pl.BlockSpec(pair.shape, lambda i: (0, 0, 0, 0)),
            pl.BlockSpec(rowmask.shape, lambda i: (0, 0, 0)),
        ],
        out_specs=pl.BlockSpec((NA_QBLK, NA_WIDTH), lambda i: (i, 0)),
        out_shape=jax.ShapeDtypeStruct((t, NA_WIDTH), BF16),
        compiler_params=_params("parallel"),
        name="na_attention",
    )(z, z, z, z, z, z, z, pair, rowmask)


FN_N2 = 128


def _dft_cos_sin(n):
    k = np.arange(n, dtype=np.float64)
    ang = 2.0 * np.pi * np.outer(k, k) / n
    return np.cos(ang), np.sin(ang)


def _fnet_a_kernel(f_ref, u_ref, y_ref):
    y_ref[...] = jnp.dot(f_ref[...], u_ref[...].astype(F32), preferred_element_type=F32, precision=HIGHEST)


def _fnet_bc_kernel(kb, yr_ref, yi_ref, twr_ref, twi_ref, m2_ref, mc_ref, fw_ref, o_ref):
    n2 = FN_N2
    for j in range(kb):
        yr, yi = yr_ref[j], yi_ref[j]
        twr = jnp.concatenate([twr_ref[j]] * FN_GROUPS, axis=-1)
        twi = jnp.concatenate([twi_ref[j]] * FN_GROUPS, axis=-1)
        z = jnp.concatenate([yr * twr - yi * twi, yr * twi + yi * twr], axis=0)
        w = jnp.dot(m2_ref[...], z, preferred_element_type=F32, precision=HIGHEST)
        outs = []
        for g in range(FN_GROUPS):
            sl = slice(g * FN_GROUP_DIM, (g + 1) * FN_GROUP_DIM)
            wg = jnp.concatenate([w[:n2, sl], w[n2:, sl]], axis=-1)
            fg = jnp.dot(wg, mc_ref[...], preferred_element_type=F32, precision=HIGHEST)
            outs.append(jnp.dot(fg.astype(BF16), fw_ref[g], preferred_element_type=F32).astype(BF16))
        o_ref[:, j * FN_WIDTH:(j + 1) * FN_WIDTH] = jnp.concatenate(outs, axis=-1)


def fnet_mix(u, fnet_w, kb=4, cw=4096):
    bsz, s, width = u.shape
    n2 = FN_N2
    n1 = s // n2
    assert n1 * n2 == s and n1 % kb == 0
    c1, s1 = _dft_cos_sin(n1)
    fa = jnp.asarray(np.concatenate([c1, -s1], axis=0), F32)
    cols = n2 * width
    cw = _row_tile(cols, cw)
    y = pl.pallas_call(
        _fnet_a_kernel,
        grid=(bsz, cols // cw),
        in_specs=[pl.BlockSpec((2 * n1, n1), lambda b, j: (0, 0)),
                  pl.BlockSpec((None, n1, cw), lambda b, j: (b, 0, j))],
        out_specs=pl.BlockSpec((None, 2 * n1, cw), lambda b, j: (b, 0, j)),
        out_shape=jax.ShapeDtypeStruct((bsz, 2 * n1, cols), F32),
        compiler_params=_params("parallel", "parallel"),
        name="fnet_stage_a",
    )(fa, u.reshape(bsz, n1, cols))
    y = y.reshape(bsz, 2, n1, n2, width)
    ang = 2.0 * np.pi * np.outer(np.arange(n1, dtype=np.float64), np.arange(n2, dtype=np.float64)) / s
    twr = jnp.asarray(np.broadcast_to(np.cos(ang)[..., None], (n1, n2, FN_GROUP_DIM)), F32)
    twi = jnp.asarray(np.broadcast_to(-np.sin(ang)[..., None], (n1, n2, FN_GROUP_DIM)), F32)
    c2, s2 = _dft_cos_sin(n2)
    m2 = jnp.asarray(np.block([[c2, s2], [-s2, c2]]), F32)
    cc, sc = _dft_cos_sin(FN_GROUP_DIM)
    mc = jnp.asarray(np.concatenate([cc, sc], axis=0) / math.sqrt(s * FN_GROUP_DIM), F32)
    out = pl.pallas_call(
        functools.partial(_fnet_bc_kernel, kb),
        grid=(bsz, n1 // kb),
        in_specs=[pl.BlockSpec((None, None, kb, n2, width), lambda b, k: (b, 0, k, 0, 0)),
                  pl.BlockSpec((None, None, kb, n2, width), lambda b, k: (b, 1, k, 0, 0)),
                  pl.BlockSpec((kb, n2, FN_GROUP_DIM), lambda b, k: (k, 0, 0)),
                  pl.BlockSpec((kb, n2, FN_GROUP_DIM), lambda b, k: (k, 0, 0)),
                  pl.BlockSpec(m2.shape, lambda b, k: (0, 0)),
                  pl.BlockSpec(mc.shape, lambda b, k: (0, 0)),
                  pl.BlockSpec(fnet_w.shape, lambda b, k: (0, 0, 0))],
        out_specs=pl.BlockSpec((None, n2, kb * width), lambda b, k: (b, 0, k)),
        out_shape=jax.ShapeDtypeStruct((bsz, n2, n1 * width), BF16),
        compiler_params=_params("parallel", "parallel"),
        name="fnet_stage_bc",
    )(y, y, twr, twi, m2, mc, fnet_w)
    return out.reshape(bsz, s, width)


def _post_mix_kernel(x_ref, a_ref, f_ref, c_ref, wout_ref, gq_ref, wq_ref, kv_ref, wo_ref, gf_ref, wr_ref,
                     x2_ref, h3_ref, aff_ref):
    x1 = x_ref[...]
    x1 = x1 + jnp.dot(a_ref[...], wout_ref[0:NA_WIDTH], preferred_element_type=F32)
    x1 = x1 + jnp.dot(f_ref[...], wout_ref[NA_WIDTH:NA_WIDTH + FN_WIDTH], preferred_element_type=F32)
    x1 = x1 + jnp.dot(c_ref[...], wout_ref[NA_WIDTH + FN_WIDTH:], preferred_element_type=F32)
    hq = _rms(x1, gq_ref[...]).astype(BF16)
    q = jnp.dot(hq, wq_ref[...], preferred_element_type=F32).astype(BF16)
    kv = kv_ref[...]
    scale = HEAD_DIM ** -0.5
    heads = []
    for h in range(MEM_HEADS):
        qh = q[:, h * HEAD_DIM:(h + 1) * HEAD_DIM]
        kh = kv[:, h * HEAD_DIM:(h + 1) * HEAD_DIM]
        vh = kv[:, MEM_WIDTH + h * HEAD_DIM:MEM_WIDTH + (h + 1) * HEAD_DIM]
        s = lax.dot_general(qh, kh, (((1,), (1,)), ((), ())), preferred_element_type=F32) * scale
        e = jnp.exp(s - jnp.max(s, axis=-1, keepdims=True))
        p = e / jnp.sum(e, axis=-1, keepdims=True)
        heads.append(jnp.dot(p.astype(BF16), vh, preferred_element_type=F32).astype(BF16))
    o = jnp.concatenate(heads, axis=-1)
    x2 = x1 + jnp.dot(o, wo_ref[...], preferred_element_type=F32)
    x2_ref[...] = x2
    h3 = _rms(x2, gf_ref[...]).astype(BF16)
    h3_ref[...] = h3
    logits = jnp.dot(h3, wr_ref[...], preferred_element_type=F32)
    el = jnp.exp(logits - jnp.max(logits, axis=-1, keepdims=True))
    aff_ref[...] = el / jnp.sum(el, axis=-1, keepdims=True)


def post_mix(x, a, f, c, w_out, g_q, wq, kv, wo, g_f, w_router, seq_lens, tm=512):
    rows, d = x.shape
    n_mem = kv.shape[0] // len(seq_lens)
    n_exp = w_router.shape[1]
    tm = _row_tile(math.gcd(*seq_lens), tm)
    bounds = np.cumsum(seq_lens)[:-1] // tm

    def seq_of(i):
        s = jnp.int32(0)
        for b in bounds:
            s = s + (i >= int(b)).astype(jnp.int32)
        return s

    const = lambda i: (0, 0)
    row = lambda width: pl.BlockSpec((tm, width), lambda i: (i, 0))
    return pl.pallas_call(
        _post_mix_kernel,
        grid=(rows // tm,),
        in_specs=[
            row(d), row(a.shape[1]), row(f.shape[1]), row(c.shape[1]),
            pl.BlockSpec(w_out.shape, const),
            pl.BlockSpec((1, d), const),
            pl.BlockSpec(wq.shape, const),
            pl.BlockSpec((n_mem, kv.shape[1]), lambda i: (seq_of(i), 0)),
            pl.BlockSpec(wo.shape, const),
            pl.BlockSpec((1, d), const),
            pl.BlockSpec(w_router.shape, const),
        ],
        out_specs=[row(d), row(d), row(n_exp)],
        out_shape=[
            jax.ShapeDtypeStruct((rows, d), F32),
            jax.ShapeDtypeStruct((rows, d), BF16),
            jax.ShapeDtypeStruct((rows, n_exp), F32),
        ],
        compiler_params=_params("parallel"),
        name="post_mix",
    )(x, a, f, c, w_out, g_q.reshape(1, d), wq, kv, wo, g_f.reshape(1, d), w_router)


def _expert_ffn_kernel(n_hidden_steps, fk, xe_ref, gate_ref, wg_ref, wu_ref, wd_ref, y_ref, hid_ref):
    s = pl.program_id(2)

    @pl.when(s < n_hidden_steps)
    def _():
        xe = xe_ref[0]
        a = jnp.dot(xe, wg_ref[0].astype(BF16), preferred_element_type=F32)
        b = jnp.dot(xe, wu_ref[0].astype(BF16), preferred_element_type=F32)
        hid_ref[:, pl.ds(pl.multiple_of(s * fk, fk), fk)] = (a * jax.nn.sigmoid(a) * b).astype(BF16)

    @pl.when(s >= n_hidden_steps)
    def _():
        y = jnp.dot(hid_ref[...], wd_ref[0].astype(BF16), preferred_element_type=F32)
        y_ref[0] = y * gate_ref[0]


def expert_ffn(xe, gates, w_gate, w_up, w_down, tm=1024, fk=512, nk=256):
    n_exp, rows, d = xe.shape
    d_ff = w_gate.shape[2]
    tm = _row_tile(rows, tm)
    fk = _row_tile(d_ff, fk)
    nk = _row_tile(d, nk)
    nf, nn = d_ff // fk, d // nk
    hid_step = lambda s: jnp.minimum(s, nf - 1)
    out_step = lambda s: jnp.maximum(s - nf, 0)
    return pl.pallas_call(
        functools.partial(_expert_ffn_kernel, nf, fk),
        grid=(n_exp, rows // tm, nf + nn),
        in_specs=[
            pl.BlockSpec((1, tm, d), lambda e, m, s: (e, m, 0)),
            pl.BlockSpec((1, tm, 1), lambda e, m, s: (e, m, 0)),
            pl.BlockSpec((1, d, fk), lambda e, m, s: (e, 0, hid_step(s))),
            pl.BlockSpec((1, d, fk), lambda e, m, s: (e, 0, hid_step(s))),
            pl.BlockSpec((1, d_ff, nk), lambda e, m, s: (e, 0, out_step(s))),
        ],
        out_specs=pl.BlockSpec((1, tm, nk), lambda e, m, s: (e, m, out_step(s))),
        out_shape=jax.ShapeDtypeStruct((n_exp, rows, d), F32),
        scratch_shapes=[pltpu.VMEM((tm, d_ff), BF16)],
        compiler_params=_params("parallel", "parallel", "arbitrary"),
        name="expert_ffn",
    )(xe, gates, w_gate, w_up, w_down)


def _final_norm_kernel(x_ref, g_ref, o_ref):
    o_ref[...] = _rms(x_ref[...], g_ref[...])


def final_norm(x, g, row0, rows, tm=1024):
    d = x.shape[1]
    tm = _row_tile(math.gcd(rows, row0) if row0 else rows, tm)
    blk0 = row0 // tm
    return pl.pallas_call(
        _final_norm_kernel,
        grid=(rows // tm,),
        in_specs=[pl.BlockSpec((tm, d), lambda i: (i + blk0, 0)), pl.BlockSpec((1, d), lambda i: (0, 0))],
        out_specs=pl.BlockSpec((tm, d), lambda i: (i, 0)),
        out_shape=jax.ShapeDtypeStruct((rows, d), F32),
        compiler_params=_params("parallel"),
        name="final_norm",
    )(x, g.reshape(1, d))


def kernel(x_prompt, x_sample, mem_prompt, mem_sample, w_in, w_out, na_bias, fnet_w, sink_logit,
           wq_mem, wkv_mem, wo_mem, w_router, w_gate, w_up, w_down,
           g_mix, g_mem_q, g_mem_kv, g_ffn, g_final):
    d = x_prompt.shape[-1]
    depth = w_in.shape[0]
    n_exp = w_router.shape[-1]
    groups = [(x_prompt.shape[0], x_prompt.shape[1]), (x_sample.shape[0], x_sample.shape[1])]
    seq_lens = [s for b, s in groups for _ in range(b)]
    group_tokens = [b * s for b, s in groups]
    caps = [EC_CAPACITY_FACTOR * n // n_exp for n in group_tokens]
    offs = np.concatenate([[0], np.cumsum(group_tokens)]).tolist()

    x = jnp.concatenate([x_prompt.reshape(-1, d), x_sample.reshape(-1, d)], axis=0)
    mem = jnp.concatenate([mem_prompt.reshape(-1, d), mem_sample.reshape(-1, d)], axis=0)

    w_in_b, w_out_b = w_in.astype(BF16), w_out.astype(BF16)
    wq_b, wkv_b, wo_b = wq_mem.astype(BF16), wkv_mem.astype(BF16), wo_mem.astype(BF16)
    wr_b, fnet_b = w_router.astype(BF16), fnet_w.astype(BF16)

    for l in range(depth):
        z = norm_matmul(x, g_mix[l], w_in_b[l], BF16)
        a = na_attention(z, na_bias[l], seq_lens)
        c = swa_attention(z, sink_logit[l], seq_lens)
        u = z[:, Z_FN:Z_FN + FN_WIDTH]
        f = jnp.concatenate(
            [fnet_mix(u[offs[gi]:offs[gi + 1]].reshape(b, s, FN_WIDTH), fnet_b[l]).reshape(b * s, FN_WIDTH)
             for gi, (b, s) in enumerate(groups)], axis=0)
        kv = norm_matmul(mem, g_mem_kv[l], wkv_b[l], BF16)
        x2, h3, aff = post_mix(x, a, f, c, w_out_b[l], g_mem_q[l], wq_b[l], kv, wo_b[l], g_ffn[l],
                               wr_b[l], seq_lens)
        gts, idxs = [], []
        for gi in range(len(groups)):
            g_, i_ = lax.top_k(aff[offs[gi]:offs[gi + 1]].T, caps[gi])
            gts.append(g_)
            idxs.append(i_ + offs[gi])
        idx = jnp.concatenate(idxs, axis=1)
        gate = jnp.concatenate(gts, axis=1)[..., None]
        y = expert_ffn(h3[idx], gate, w_gate[l], w_up[l], w_down[l])
        x = x2.at[idx.reshape(-1)].add(y.reshape(-1, d))

    y_prompt = final_norm(x, g_final, offs[0], group_tokens[0]).reshape(x_prompt.shape)
    y_sample = final_norm(x, g_final, offs[1], group_tokens[1]).reshape(x_sample.shape)
    return (y_prompt, y_sample)
```

```python
import functools
import math

import jax
import jax.numpy as jnp
import numpy as np
from jax import lax
from jax.experimental import pallas as pl
from jax.experimental.pallas import tpu as pltpu

HEAD_DIM = 128
NA_HEADS = 4
NA_WIDTH = NA_HEADS * HEAD_DIM
FN_GROUPS = 4
FN_GROUP_DIM = 128
FN_WIDTH = FN_GROUPS * FN_GROUP_DIM
SWA_Q_HEADS = 8
SWA_KV_HEADS = 2
SWA_GROUP = SWA_Q_HEADS // SWA_KV_HEADS
SWA_WIDTH = SWA_Q_HEADS * HEAD_DIM
SWA_KV_WIDTH = SWA_KV_HEADS * HEAD_DIM
GRID_W = 64
NA_ROWS = 8
NA_COLS = 16
SWA_WINDOW = 128
SWA_BLOCK = 128
MEM_HEADS = 4
MEM_WIDTH = MEM_HEADS * HEAD_DIM
EC_CAPACITY_FACTOR = 2
RMS_EPS = 1e-6
NEG_INF = -1e30

Z_NA_Q, Z_NA_K, Z_NA_V = 0, NA_WIDTH, 2 * NA_WIDTH
Z_FN = 3 * NA_WIDTH
Z_SWA_Q = Z_FN + FN_WIDTH
Z_SWA_K = Z_SWA_Q + SWA_WIDTH
Z_SWA_V = Z_SWA_K + SWA_KV_WIDTH

V7X_VMEM_LIMIT_BYTES = 56 * 1024 * 1024

BF16 = jnp.bfloat16
F32 = jnp.float32
HIGHEST = lax.Precision.HIGHEST


def _params(*semantics):
    return pltpu.CompilerParams(dimension_semantics=semantics, vmem_limit_bytes=V7X_VMEM_LIMIT_BYTES)


def _rms(x, g):
    ms = jnp.mean(x * x, axis=-1, keepdims=True)
    return x * lax.rsqrt(ms + RMS_EPS) * g


def _row_tile(rows, want):
    t = min(rows, want)
    while rows % t:
        t //= 2
    return t


def _edge_fns(seq_lens, blk):
    firsts, lasts = [], []
    o = 0
    for s in seq_lens:
        firsts.append(o // blk)
        o += s
        lasts.append(o // blk - 1)

    def first_of(i):
        f = jnp.int32(firsts[0])
        for b in firsts[1:]:
            f = jnp.where(i >= b, jnp.int32(b), f)
        return f

    def last_of(i):
        l = jnp.int32(lasts[-1])
        for b in reversed(lasts[:-1]):
            l = jnp.where(i <= b, jnp.int32(b), l)
        return l

    return first_of, last_of


def _norm_matmul_kernel(x_ref, g_ref, w_ref, o_ref):
    h = _rms(x_ref[...], g_ref[...]).astype(BF16)
    o_ref[...] = jnp.dot(h, w_ref[...], preferred_element_type=F32).astype(o_ref.dtype)


def norm_matmul(x, g, w, out_dtype, tm=512):
    rows, d = x.shape
    n = w.shape[1]
    tm = _row_tile(rows, tm)
    return pl.pallas_call(
        _norm_matmul_kernel,
        grid=(rows // tm,),
        in_specs=[
            pl.BlockSpec((tm, d), lambda i: (i, 0)),
            pl.BlockSpec((1, d), lambda i: (0, 0)),
            pl.BlockSpec((d, n), lambda i: (0, 0)),
        ],
        out_specs=pl.BlockSpec((tm, n), lambda i: (i, 0)),
        out_shape=jax.ShapeDtypeStruct((rows, n), out_dtype),
        compiler_params=_params("parallel"),
        name="norm_matmul",
    )(x, g.reshape(1, d), w)


def _swa_kernel(first_of, last_of, q_ref, kp_ref, kc_ref, kn_ref, vp_ref, vc_ref, vn_ref, pen_ref,
                sink_ref, o_ref):
    i = pl.program_id(0)
    is_first = i == first_of(i)
    is_last = i == last_of(i)
    q = q_ref[...]
    kp, kc, kn = kp_ref[...], kc_ref[...], kn_ref[...]
    vp, vc, vn = vp_ref[...], vc_ref[...], vn_ref[...]
    scale = HEAD_DIM ** -0.5
    blk = SWA_BLOCK
    outs = []
    for g in range(SWA_KV_HEADS):
        sl = slice(g * HEAD_DIM, (g + 1) * HEAD_DIM)
        qg = jnp.concatenate([q[:, (g * SWA_GROUP + j) * HEAD_DIM:(g * SWA_GROUP + j + 1) * HEAD_DIM]
                              for j in range(SWA_GROUP)], axis=0)
        kw = jnp.concatenate([kp[:, sl], kc[:, sl], kn[:, sl]], axis=0)
        vw = jnp.concatenate([vp[:, sl], vc[:, sl], vn[:, sl]], axis=0)
        s = lax.dot_general(qg, kw, (((1,), (1,)), ((), ())), preferred_element_type=F32)
        s = s * scale - pen_ref[g]
        col = lax.broadcasted_iota(jnp.int32, s.shape, 1)
        outside = (is_first & (col < blk)) | (is_last & (col >= 2 * blk))
        s = jnp.where(outside, NEG_INF, s)
        sk = sink_ref[g]
        m = jnp.maximum(jnp.max(s, axis=-1, keepdims=True), sk)
        e = jnp.exp(s - m)
        den = jnp.sum(e, axis=-1, keepdims=True) + jnp.exp(sk - m)
        p = (e / den).astype(BF16)
        og = jnp.dot(p, vw, preferred_element_type=F32).astype(BF16)
        outs.extend([og[j * blk:(j + 1) * blk] for j in range(SWA_GROUP)])
    o_ref[...] = jnp.concatenate(outs, axis=-1)


def swa_attention(z, sink_logit, seq_lens):
    t = z.shape[0]
    blk = SWA_BLOCK
    assert all(s % blk == 0 for s in seq_lens)
    first_of, last_of = _edge_fns(seq_lens, blk)
    qcol, kcol, vcol = Z_SWA_Q // SWA_WIDTH, Z_SWA_K // SWA_KV_WIDTH, Z_SWA_V // SWA_KV_WIDTH
    assert qcol * SWA_WIDTH == Z_SWA_Q and kcol * SWA_KV_WIDTH == Z_SWA_K
    qi = np.arange(blk)[:, None]
    kj = np.arange(3 * blk)[None, :]
    rel = np.abs(kj - blk - qi).astype(np.float32)
    slopes = jnp.exp2(-jnp.arange(1, SWA_Q_HEADS + 1, dtype=F32) * (8.0 / SWA_Q_HEADS))
    pen = jnp.where(jnp.asarray(rel <= SWA_WINDOW)[None], slopes[:, None, None] * rel[None], -NEG_INF)
    pen = pen.reshape(SWA_KV_HEADS, SWA_GROUP * blk, 3 * blk)
    sink = jnp.repeat(sink_logit.astype(F32), blk).reshape(SWA_KV_HEADS, SWA_GROUP * blk, 1)
    prev = lambda i: jnp.maximum(i - 1, first_of(i))
    nxt = lambda i: jnp.minimum(i + 1, last_of(i))
    same = lambda i: i
    kv = lambda col, f: pl.BlockSpec((blk, SWA_KV_WIDTH), lambda i: (f(i), col))
    return pl.pallas_call(
        functools.partial(_swa_kernel, first_of, last_of),
        grid=(t // blk,),
        in_specs=[
            pl.BlockSpec((blk, SWA_WIDTH), lambda i: (i, qcol)),
            kv(kcol, prev), kv(kcol, same), kv(kcol, nxt),
            kv(vcol, prev), kv(vcol, same), kv(vcol, nxt),
            pl.BlockSpec(pen.shape, lambda i: (0, 0, 0)),
            pl.BlockSpec(sink.shape, lambda i: (0, 0, 0)),
        ],
        out_specs=pl.BlockSpec((blk, SWA_WIDTH), lambda i: (i, 0)),
        out_shape=jax.ShapeDtypeStruct((t, SWA_WIDTH), BF16),
        compiler_params=_params("parallel"),
        name="swa_attention",
    )(z, z, z, z, z, z, z, pen, sink)


NA_QROWS = 4
NA_QBLK = NA_QROWS * GRID_W
NA_PAIRS = 3 * NA_QROWS // 2


def _na_kernel(first_of, last_of, q_ref, kp_ref, kc_ref, kn_ref, vp_ref, vc_ref, vn_ref, bias_ref,
               rowmask_ref, o_ref):
    b = pl.program_id(0)
    first = first_of(b)
    n_rows = NA_QROWS * (last_of(b) - first + 1)
    q = q_ref[...]
    kp, kc, kn = kp_ref[...], kc_ref[...], kn_ref[...]
    vp, vc, vn = vp_ref[...], vc_ref[...], vn_ref[...]
    scale = HEAD_DIM ** -0.5
    codes = []
    for i in range(NA_QROWS):
        r = NA_QROWS * (b - first) + i
        r0 = jnp.clip(r - NA_ROWS // 2, 0, n_rows - NA_ROWS)
        v = r - r0
        in_window = lambda dr: ((dr >= NA_ROWS - 1 - v) & (dr < 2 * NA_ROWS - 1 - v)).astype(jnp.int32)
        codes.append([in_window(2 * p - i + NA_QROWS - 1) + 2 * in_window(2 * p - i + NA_QROWS)
                      for p in range(NA_PAIRS)])
    outs = []
    for h in range(NA_HEADS):
        sl = slice(h * HEAD_DIM, (h + 1) * HEAD_DIM)
        kw = jnp.concatenate([kp[:, sl], kc[:, sl], kn[:, sl]], axis=0)
        vw = jnp.concatenate([vp[:, sl], vc[:, sl], vn[:, sl]], axis=0)
        s = lax.dot_general(q[:, sl], kw, (((1,), (1,)), ((), ())), preferred_element_type=F32) * scale
        probs = []
        for i in range(NA_QROWS):
            tiles = []
            for p in range(NA_PAIRS):
                dr0 = 2 * p - i + NA_QROWS - 1
                t = s[i * GRID_W:(i + 1) * GRID_W, p * 2 * GRID_W:(p + 1) * 2 * GRID_W]
                tiles.append(t + bias_ref[h, dr0] + rowmask_ref[codes[i][p]])
            si = jnp.concatenate(tiles, axis=-1)
            e = jnp.exp(si - jnp.max(si, axis=-1, keepdims=True))
            probs.append((e / jnp.sum(e, axis=-1, keepdims=True)).astype(BF16))
        pm = jnp.concatenate(probs, axis=0)
        outs.append(jnp.dot(pm, vw, preferred_element_type=F32).astype(BF16))
    o_ref[...] = jnp.concatenate(outs, axis=-1)


def _na_tables(na_bias):
    c = np.arange(GRID_W)
    c0 = np.clip(c - NA_COLS // 2, 0, GRID_W - NA_COLS)
    in_col = (c[None, :] >= c0[:, None]) & (c[None, :] < c0[:, None] + NA_COLS)
    dc_idx = np.clip(c[None, :] - c[:, None], -(NA_COLS - 1), NA_COLS - 1) + NA_COLS - 1
    tab = na_bias.astype(F32)[:, :, dc_idx]
    tab = jnp.where(jnp.asarray(in_col)[None, None], tab, NEG_INF)
    pair = jnp.concatenate([tab[:, :-1], tab[:, 1:]], axis=-1)
    left = np.arange(2 * GRID_W) < GRID_W
    masks = [np.broadcast_to(np.where(np.where(left, code & 1, code >> 1) > 0, 0.0, NEG_INF), (GRID_W, 2 * GRID_W))
             for code in range(4)]
    return pair, jnp.asarray(np.stack(masks), F32)


def na_attention(z, na_bias, seq_lens):
    t = z.shape[0]
    assert all(s % NA_QBLK == 0 and s // GRID_W >= 2 * NA_ROWS for s in seq_lens)
    assert 2 * NA_QROWS == NA_ROWS
    first_of, last_of = _edge_fns(seq_lens, NA_QBLK)
    pair, rowmask = _na_tables(na_bias)
    prev = lambda i: jnp.maximum(i - 1, first_of(i))
    nxt = lambda i: jnp.minimum(i + 1, last_of(i))
    same = lambda i: i
    blk = lambda col, f: pl.BlockSpec((NA_QBLK, NA_WIDTH), lambda i: (f(i), col))
    qcol, kcol, vcol = Z_NA_Q // NA_WIDTH, Z_NA_K // NA_WIDTH, Z_NA_V // NA_WIDTH
    return pl.pallas_call(
        functools.partial(_na_kernel, first_of, last_of),
        grid=(t // NA_QBLK,),
        in_specs=[
            blk(qcol, same),
            blk(kcol, prev), blk(kcol, same), blk(kcol, nxt),
            blk(vcol, prev), blk(vcol, same), blk(vcol, nxt),
            pl.BlockSpec(pair.shape, lambda i: (0, 0, 0, 0)),
            pl.BlockSpec(rowmask.shape, lambda i: (0, 0, 0)),
        ],
        out_specs=pl.BlockSpec((NA_QBLK, NA_WIDTH), lambda i: (i, 0)),
        out_shape=jax.ShapeDtypeStruct((t, NA_WIDTH), BF16),
        compiler_params=_params("parallel"),
        name="na_attention",
    )(z, z, z, z, z, z, z, pair, rowmask)


FN_N2 = 128


def _dft_cos_sin(n):
    k = np.arange(n, dtype=np.float64)
    ang = 2.0 * np.pi * np.outer(k, k) / n
    return np.cos(ang), np.sin(ang)


def _fnet_a_kernel(f_ref, u_ref, y_ref):
    y_ref[...] = jnp.dot(f_ref[...], u_ref[...].astype(F32), preferred_element_type=F32, precision=HIGHEST)


def _fnet_bc_kernel(kb, yr_ref, yi_ref, twr_ref, twi_ref, m2_ref, mc_ref, fw_ref, o_ref):
    n2 = FN_N2
    for j in range(kb):
        yr, yi = yr_ref[j], yi_ref[j]
        twr = jnp.concatenate([twr_ref[j]] * FN_GROUPS, axis=-1)
        twi = jnp.concatenate([twi_ref[j]] * FN_GROUPS, axis=-1)
        z = jnp.concatenate([yr * twr - yi * twi, yr * twi + yi * twr], axis=0)
        w = jnp.dot(m2_ref[...], z, preferred_element_type=F32, precision=HIGHEST)
        outs = []
        for g in range(FN_GROUPS):
            sl = slice(g * FN_GROUP_DIM, (g + 1) * FN_GROUP_DIM)
            wg = jnp.concatenate([w[:n2, sl], w[n2:, sl]], axis=-1)
            fg = jnp.dot(wg, mc_ref[...], preferred_element_type=F32, precision=HIGHEST)
            outs.append(jnp.dot(fg.astype(BF16), fw_ref[g], preferred_element_type=F32).astype(BF16))
        o_ref[:, j * FN_WIDTH:(j + 1) * FN_WIDTH] = jnp.concatenate(outs, axis=-1)


def fnet_mix(u, fnet_w, kb=4, cw=4096):
    bsz, s, width = u.shape
    n2 = FN_N2
    n1 = s // n2
    assert n1 * n2 == s and n1 % kb == 0
    c1, s1 = _dft_cos_sin(n1)
    fa = jnp.asarray(np.concatenate([c1, -s1], axis=0), F32)
    cols = n2 * width
    cw = _row_tile(cols, cw)
    y = pl.pallas_call(
        _fnet_a_kernel,
        grid=(bsz, cols // cw),
        in_specs=[pl.BlockSpec((2 * n1, n1), lambda b, j: (0, 0)),
                  pl.BlockSpec((None, n1, cw), lambda b, j: (b, 0, j))],
        out_specs=pl.BlockSpec((None, 2 * n1, cw), lambda b, j: (b, 0, j)),
        out_shape=jax.ShapeDtypeStruct((bsz, 2 * n1, cols), F32),
        compiler_params=_params("parallel", "parallel"),
        name="fnet_stage_a",
    )(fa, u.reshape(bsz, n1, cols))
    y = y.reshape(bsz, 2, n1, n2, width)
    ang = 2.0 * np.pi * np.outer(np.arange(n1, dtype=np.float64), np.arange(n2, dtype=np.float64)) / s
    twr = jnp.asarray(np.broadcast_to(np.cos(ang)[..., None], (n1, n2, FN_GROUP_DIM)), F32)
    twi = jnp.asarray(np.broadcast_to(-np.sin(ang)[..., None], (n1, n2, FN_GROUP_DIM)), F32)
    c2, s2 = _dft_cos_sin(n2)
    m2 = jnp.asarray(np.block([[c2, s2], [-s2, c2]]), F32)
    cc, sc = _dft_cos_sin(FN_GROUP_DIM)
    mc = jnp.asarray(np.concatenate([cc, sc], axis=0) / math.sqrt(s * FN_GROUP_DIM), F32)
    out = pl.pallas_call(
        functools.partial(_fnet_bc_kernel, kb),
        grid=(bsz, n1 // kb),
        in_specs=[pl.BlockSpec((None, None, kb, n2, width), lambda b, k: (b, 0, k, 0, 0)),
                  pl.BlockSpec((None, None, kb, n2, width), lambda b, k: (b, 1, k, 0, 0)),
                  pl.BlockSpec((kb, n2, FN_GROUP_DIM), lambda b, k: (k, 0, 0)),
                  pl.BlockSpec((kb, n2, FN_GROUP_DIM), lambda b, k: (k, 0, 0)),
                  pl.BlockSpec(m2.shape, lambda b, k: (0, 0)),
                  pl.BlockSpec(mc.shape, lambda b, k: (0, 0)),
                  pl.BlockSpec(fnet_w.shape, lambda b, k: (0, 0, 0))],
        out_specs=pl.BlockSpec((None, n2, kb * width), lambda b, k: (b, 0, k)),
        out_shape=jax.ShapeDtypeStruct((bsz, n2, n1 * width), BF16),
        compiler_params=_params("parallel", "parallel"),
        name="fnet_stage_bc",
    )(y, y, twr, twi, m2, mc, fnet_w)
    return out.reshape(bsz, s, width)


def _post_mix_kernel(x_ref, a_ref, f_ref, c_ref, wout_ref, gq_ref, wq_ref, kv_ref, wo_ref, gf_ref, wr_ref,
                     x2_ref, h3_ref, aff_ref):
    x1 = x_ref[...]
    x1 = x1 + jnp.dot(a_ref[...], wout_ref[0:NA_WIDTH], preferred_element_type=F32)
    x1 = x1 + jnp.dot(f_ref[...], wout_ref[NA_WIDTH:NA_WIDTH + FN_WIDTH], preferred_element_type=F32)
    x1 = x1 + jnp.dot(c_ref[...], wout_ref[NA_WIDTH + FN_WIDTH:], preferred_element_type=F32)
    hq = _rms(x1, gq_ref[...]).astype(BF16)
    q = jnp.dot(hq, wq_ref[...], preferred_element_type=F32).astype(BF16)
    kv = kv_ref[...]
    scale = HEAD_DIM ** -0.5
    heads = []
    for h in range(MEM_HEADS):
        qh = q[:, h * HEAD_DIM:(h + 1) * HEAD_DIM]
        kh = kv[:, h * HEAD_DIM:(h + 1) * HEAD_DIM]
        vh = kv[:, MEM_WIDTH + h * HEAD_DIM:MEM_WIDTH + (h + 1) * HEAD_DIM]
        s = lax.dot_general(qh, kh, (((1,), (1,)), ((), ())), preferred_element_type=F32) * scale
        e = jnp.exp(s - jnp.max(s, axis=-1, keepdims=True))
        p = e / jnp.sum(e, axis=-1, keepdims=True)
        heads.append(jnp.dot(p.astype(BF16), vh, preferred_element_type=F32).astype(BF16))
    o = jnp.concatenate(heads, axis=-1)
    x2 = x1 + jnp.dot(o, wo_ref[...], preferred_element_type=F32)
    x2_ref[...] = x2
    h3 = _rms(x2, gf_ref[...]).astype(BF16)
    h3_ref[...] = h3
    logits = jnp.dot(h3, wr_ref[...], preferred_element_type=F32)
    el = jnp.exp(logits - jnp.max(logits, axis=-1, keepdims=True))
    aff_ref[...] = el / jnp.sum(el, axis=-1, keepdims=True)


def post_mix(x, a, f, c, w_out, g_q, wq, kv, wo, g_f, w_router, seq_lens, tm=512):
    rows, d = x.shape
    n_mem = kv.shape[0] // len(seq_lens)
    n_exp = w_router.shape[1]
    tm = _row_tile(math.gcd(*seq_lens), tm)
    bounds = np.cumsum(seq_lens)[:-1] // tm

    def seq_of(i):
        s = jnp.int32(0)
        for b in bounds:
            s = s + (i >= int(b)).astype(jnp.int32)
        return s

    const = lambda i: (0, 0)
    row = lambda width: pl.BlockSpec((tm, width), lambda i: (i, 0))
    return pl.pallas_call(
        _post_mix_kernel,
        grid=(rows // tm,),
        in_specs=[
            row(d), row(a.shape[1]), row(f.shape[1]), row(c.shape[1]),
            pl.BlockSpec(w_out.shape, const),
            pl.BlockSpec((1, d), const),
            pl.BlockSpec(wq.shape, const),
            pl.BlockSpec((n_mem, kv.shape[1]), lambda i: (seq_of(i), 0)),
            pl.BlockSpec(wo.shape, const),
            pl.BlockSpec((1, d), const),
            pl.BlockSpec(w_router.shape, const),
        ],
        out_specs=[row(d), row(d), row(n_exp)],
        out_shape=[
            jax.ShapeDtypeStruct((rows, d), F32),
            jax.ShapeDtypeStruct((rows, d), BF16),
            jax.ShapeDtypeStruct((rows, n_exp), F32),
        ],
        compiler_params=_params("parallel"),
        name="post_mix",
    )(x, a, f, c, w_out, g_q.reshape(1, d), wq, kv, wo, g_f.reshape(1, d), w_router)


def _expert_ffn_kernel(n_hidden_steps, fk, xe_ref, gate_ref, wg_ref, wu_ref, wd_ref, y_ref, hid_ref):
    s = pl.program_id(2)

    @pl.when(s < n_hidden_steps)
    def _():
        xe = xe_ref[0]
        a = jnp.dot(xe, wg_ref[0].astype(BF16), preferred_element_type=F32)
        b = jnp.dot(xe, wu_ref[0].astype(BF16), preferred_element_type=F32)
        hid_ref[:, pl.ds(pl.multiple_of(s * fk, fk), fk)] = (a * jax.nn.sigmoid(a) * b).astype(BF16)

    @pl.when(s >= n_hidden_steps)
    def _():
        y = jnp.dot(hid_ref[...], wd_ref[0].astype(BF16), preferred_element_type=F32)
        y_ref[0] = (y * gate_ref[0]).astype(y_ref.dtype)


def expert_ffn(xe, gates, w_gate, w_up, w_down, layer, tm=1024, fk=512, nk=256):
    n_exp, rows, d = xe.shape
    d_ff = w_gate.shape[-1]
    tm = _row_tile(rows, tm)
    fk = _row_tile(d_ff, fk)
    nk = _row_tile(d, nk)
    nf, nn = d_ff // fk, d // nk
    hid_step = lambda s: jnp.minimum(s, nf - 1)
    out_step = lambda s: jnp.maximum(s - nf, 0)
    return pl.pallas_call(
        functools.partial(_expert_ffn_kernel, nf, fk),
        grid=(n_exp, rows // tm, nf + nn),
        in_specs=[
            pl.BlockSpec((1, tm, d), lambda e, m, s: (e, m, 0)),
            pl.BlockSpec((1, tm, 1), lambda e, m, s: (e, m, 0)),
            pl.BlockSpec((None, 1, d, fk), lambda e, m, s: (layer, e, 0, hid_step(s))),
            pl.BlockSpec((None, 1, d, fk), lambda e, m, s: (layer, e, 0, hid_step(s))),
            pl.BlockSpec((None, 1, d_ff, nk), lambda e, m, s: (layer, e, 0, out_step(s))),
        ],
        out_specs=pl.BlockSpec((1, tm, nk), lambda e, m, s: (e, m, out_step(s))),
        out_shape=jax.ShapeDtypeStruct((n_exp, rows, d), BF16),
        scratch_shapes=[pltpu.VMEM((tm, d_ff), BF16)],
        compiler_params=_params("parallel", "parallel", "arbitrary"),
        name="expert_ffn",
    )(xe, gates, w_gate, w_up, w_down)


CB_TM = 256
CB_CH = 32
CB_KB = 256


def _combine_kernel(n_exp, tm, p0_ref, x2_ref, y_hbm, tok_hbm, o_ref, ybuf, tokbuf, ysem, toksem):
    i = pl.program_id(0)
    ch, kb = CB_CH, CB_KB

    @pl.when(i == 0)
    def _():
        ybuf[...] = jnp.zeros_like(ybuf)
        tokbuf[...] = jnp.full(tokbuf.shape, -1, jnp.int32)

    def y_copy(e, src, dst):
        return pltpu.make_async_copy(y_hbm.at[e, pl.ds(src, ch)], ybuf.at[pl.ds(dst, ch)], ysem)

    def tok_copy(e, src, dst):
        return pltpu.make_async_copy(tok_hbm.at[e, pl.ds(src, ch)], tokbuf.at[pl.ds(dst, ch)], toksem)

    off = jnp.int32(0)
    for e in range(n_exp):
        p0 = p0_ref[i, e]
        p1 = p0_ref[i + 1, e]
        a0 = (p0 // ch) * ch
        nch = jnp.where(p1 > p0, (p1 - a0 + ch - 1) // ch, 0)

        def issue(k, carry, e=e, a0=a0, off=off):
            src = pl.multiple_of(a0 + k * ch, ch)
            dst = pl.multiple_of(off + k * ch, ch)
            y_copy(e, src, dst).start()
            tok_copy(e, src, dst).start()
            return carry

        lax.fori_loop(0, nch, issue, 0)
        off = off + nch * ch

    def drain(k, carry):
        y_copy(0, 0, 0).wait()
        tok_copy(0, 0, 0).wait()
        return carry

    lax.fori_loop(0, off // ch, drain, 0)

    o_ref[...] = x2_ref[...]
    t0 = i * tm
    lane = lax.broadcasted_iota(jnp.int32, (kb, 128), 1)
    row = lax.broadcasted_iota(jnp.int32, (kb, 128), 0)

    def block(j, carry):
        r0 = pl.multiple_of(j * kb, kb)
        tok = jnp.where(row + r0 < off, tokbuf[pl.ds(r0, kb), :] - t0, -1)
        bt = jnp.concatenate([(tok == lane + 128 * c) for c in range(tm // 128)], axis=-1)
        bt = bt.astype(F32).astype(BF16)
        o_ref[...] += lax.dot_general(bt, ybuf[pl.ds(r0, kb), :], (((0,), (0,)), ((), ())),
                                      preferred_element_type=F32)
        return carry

    lax.fori_loop(0, (off + kb - 1) // kb, block, 0)


def moe_combine(x2, y, idx_sorted, tm=CB_TM):
    t, d = x2.shape
    n_exp, r, _ = y.shape
    assert t % tm == 0 and r % CB_CH == 0 and tm % 128 == 0
    n_tiles = t // tm
    bounds = jnp.arange(n_tiles + 1, dtype=jnp.int32) * tm
    p0 = jax.vmap(lambda row: jnp.searchsorted(row, bounds, side='left'))(idx_sorted).T.astype(jnp.int32)
    tok = jnp.broadcast_to(idx_sorted[..., None], (n_exp, r, 128))
    buf_rows = pl.cdiv(n_exp * (tm + 2 * CB_CH), CB_KB) * CB_KB
    grid_spec = pltpu.PrefetchScalarGridSpec(
        num_scalar_prefetch=1,
        grid=(n_tiles,),
        in_specs=[pl.BlockSpec((tm, d), lambda i, p: (i, 0)),
                  pl.BlockSpec(memory_space=pl.ANY),
                  pl.BlockSpec(memory_space=pl.ANY)],
        out_specs=pl.BlockSpec((tm, d), lambda i, p: (i, 0)),
        scratch_shapes=[pltpu.VMEM((buf_rows, d), BF16), pltpu.VMEM((buf_rows, 128), jnp.int32),
                        pltpu.SemaphoreType.DMA(()), pltpu.SemaphoreType.DMA(())],
    )
    return pl.pallas_call(
        functools.partial(_combine_kernel, n_exp, tm),
        grid_spec=grid_spec,
        out_shape=jax.ShapeDtypeStruct((t, d), F32),
        compiler_params=_params("arbitrary"),
        name="moe_combine",
    )(p0, x2, y, tok)


def _final_norm_kernel(x_ref, g_ref, o_ref):
    o_ref[...] = _rms(x_ref[...], g_ref[...])


def final_norm(x, g, row0, rows, tm=1024):
    d = x.shape[1]
    tm = _row_tile(math.gcd(rows, row0) if row0 else rows, tm)
    blk0 = row0 // tm
    return pl.pallas_call(
        _final_norm_kernel,
        grid=(rows // tm,),
        in_specs=[pl.BlockSpec((tm, d), lambda i: (i + blk0, 0)), pl.BlockSpec((1, d), lambda i: (0, 0))],
        out_specs=pl.BlockSpec((tm, d), lambda i: (i, 0)),
        out_shape=jax.ShapeDtypeStruct((rows, d), F32),
        compiler_params=_params("parallel"),
        name="final_norm",
    )(x, g.reshape(1, d))


def kernel(x_prompt, x_sample, mem_prompt, mem_sample, w_in, w_out, na_bias, fnet_w, sink_logit,
           wq_mem, wkv_mem, wo_mem, w_router, w_gate, w_up, w_down,
           g_mix, g_mem_q, g_mem_kv, g_ffn, g_final):
    d = x_prompt.shape[-1]
    depth = w_in.shape[0]
    n_exp = w_router.shape[-1]
    groups = [(x_prompt.shape[0], x_prompt.shape[1]), (x_sample.shape[0], x_sample.shape[1])]
    seq_lens = [s for b, s in groups for _ in range(b)]
    group_tokens = [b * s for b, s in groups]
    caps = [EC_CAPACITY_FACTOR * n // n_exp for n in group_tokens]
    offs = np.concatenate([[0], np.cumsum(group_tokens)]).tolist()

    x = jnp.concatenate([x_prompt.reshape(-1, d), x_sample.reshape(-1, d)], axis=0)
    mem = jnp.concatenate([mem_prompt.reshape(-1, d), mem_sample.reshape(-1, d)], axis=0)

    w_in_b, w_out_b = w_in.astype(BF16), w_out.astype(BF16)
    wq_b, wkv_b, wo_b = wq_mem.astype(BF16), wkv_mem.astype(BF16), wo_mem.astype(BF16)
    wr_b, fnet_b = w_router.astype(BF16), fnet_w.astype(BF16)

    for l in range(depth):
        z = norm_matmul(x, g_mix[l], w_in_b[l], BF16)
        a = na_attention(z, na_bias[l], seq_lens)
        c = swa_attention(z, sink_logit[l], seq_lens)
        u = z[:, Z_FN:Z_FN + FN_WIDTH]
        f = jnp.concatenate(
            [fnet_mix(u[offs[gi]:offs[gi + 1]].reshape(b, s, FN_WIDTH), fnet_b[l]).reshape(b * s, FN_WIDTH)
             for gi, (b, s) in enumerate(groups)], axis=0)
        kv = norm_matmul(mem, g_mem_kv[l], wkv_b[l], BF16)
        x2, h3, aff = post_mix(x, a, f, c, w_out_b[l], g_mem_q[l], wq_b[l], kv, wo_b[l], g_ffn[l],
                               wr_b[l], seq_lens)
        aff_t = aff.T
        idxs = [lax.top_k(aff_t[:, offs[gi]:offs[gi + 1]], caps[gi])[1] + offs[gi] for gi in range(len(groups))]
        idx = jnp.sort(jnp.concatenate(idxs, axis=1), axis=1)
        gate = jnp.take_along_axis(aff_t, idx, axis=1)[..., None]
        y = expert_ffn(h3[idx], gate, w_gate, w_up, w_down, l)
        x = moe_combine(x2, y, idx)

    y_prompt = final_norm(x, g_final, offs[0], group_tokens[0]).reshape(x_prompt.shape)
    y_sample = final_norm(x, g_final, offs[1], group_tokens[1]).reshape(x_sample.shape)
    return (y_prompt, y_sample)
```

```python
import functools
import math

import jax
import jax.numpy as jnp
import numpy as np
from jax import lax
from jax.experimental import pallas as pl
from jax.experimental.pallas import tpu as pltpu

HEAD_DIM = 128
NA_HEADS = 4
NA_WIDTH = NA_HEADS * HEAD_DIM
FN_GROUPS = 4
FN_GROUP_DIM = 128
FN_WIDTH = FN_GROUPS * FN_GROUP_DIM
SWA_Q_HEADS = 8
SWA_KV_HEADS = 2
SWA_GROUP = SWA_Q_HEADS // SWA_KV_HEADS
SWA_WIDTH = SWA_Q_HEADS * HEAD_DIM
SWA_KV_WIDTH = SWA_KV_HEADS * HEAD_DIM
GRID_W = 64
NA_ROWS = 8
NA_COLS = 16
SWA_WINDOW = 128
SWA_BLOCK = 128
MEM_HEADS = 4
MEM_WIDTH = MEM_HEADS * HEAD_DIM
EC_CAPACITY_FACTOR = 2
RMS_EPS = 1e-6
NEG_INF = -1e30

Z_NA_Q, Z_NA_K, Z_NA_V = 0, NA_WIDTH, 2 * NA_WIDTH
Z_FN = 3 * NA_WIDTH
Z_SWA_Q = Z_FN + FN_WIDTH
Z_SWA_K = Z_SWA_Q + SWA_WIDTH
Z_SWA_V = Z_SWA_K + SWA_KV_WIDTH

V7X_VMEM_LIMIT_BYTES = 56 * 1024 * 1024

BF16 = jnp.bfloat16
F32 = jnp.float32
HIGHEST = lax.Precision.HIGHEST


def _params(*semantics):
    return pltpu.CompilerParams(dimension_semantics=semantics, vmem_limit_bytes=V7X_VMEM_LIMIT_BYTES)


def _rms(x, g):
    ms = jnp.mean(x * x, axis=-1, keepdims=True)
    return x * lax.rsqrt(ms + RMS_EPS) * g


def _row_tile(rows, want):
    t = min(rows, want)
    while rows % t:
        t //= 2
    return t


def _edge_fns(seq_lens, blk):
    firsts, lasts = [], []
    o = 0
    for s in seq_lens:
        firsts.append(o // blk)
        o += s
        lasts.append(o // blk - 1)

    def first_of(i):
        f = jnp.int32(firsts[0])
        for b in firsts[1:]:
            f = jnp.where(i >= b, jnp.int32(b), f)
        return f

    def last_of(i):
        l = jnp.int32(lasts[-1])
        for b in reversed(lasts[:-1]):
            l = jnp.where(i <= b, jnp.int32(b), l)
        return l

    return first_of, last_of


def _norm_matmul_kernel(x_ref, g_ref, w_ref, o_ref):
    h = _rms(x_ref[...], g_ref[...]).astype(BF16)
    o_ref[...] = jnp.dot(h, w_ref[...], preferred_element_type=F32).astype(o_ref.dtype)


def norm_matmul(x, g, w, out_dtype, tm=512):
    rows, d = x.shape
    n = w.shape[1]
    tm = _row_tile(rows, tm)
    return pl.pallas_call(
        _norm_matmul_kernel,
        grid=(rows // tm,),
        in_specs=[
            pl.BlockSpec((tm, d), lambda i: (i, 0)),
            pl.BlockSpec((1, d), lambda i: (0, 0)),
            pl.BlockSpec((d, n), lambda i: (0, 0)),
        ],
        out_specs=pl.BlockSpec((tm, n), lambda i: (i, 0)),
        out_shape=jax.ShapeDtypeStruct((rows, n), out_dtype),
        compiler_params=_params("parallel"),
        name="norm_matmul",
    )(x, g.reshape(1, d), w)


def _swa_kernel(first_of, last_of, q_ref, kp_ref, kc_ref, kn_ref, vp_ref, vc_ref, vn_ref, pen_ref,
                sink_ref, o_ref):
    i = pl.program_id(0)
    is_first = i == first_of(i)
    is_last = i == last_of(i)
    q = q_ref[...]
    kp, kc, kn = kp_ref[...], kc_ref[...], kn_ref[...]
    vp, vc, vn = vp_ref[...], vc_ref[...], vn_ref[...]
    scale = HEAD_DIM ** -0.5
    blk = SWA_BLOCK
    outs = []
    for g in range(SWA_KV_HEADS):
        sl = slice(g * HEAD_DIM, (g + 1) * HEAD_DIM)
        qg = jnp.concatenate([q[:, (g * SWA_GROUP + j) * HEAD_DIM:(g * SWA_GROUP + j + 1) * HEAD_DIM]
                              for j in range(SWA_GROUP)], axis=0)
        kw = jnp.concatenate([kp[:, sl], kc[:, sl], kn[:, sl]], axis=0)
        vw = jnp.concatenate([vp[:, sl], vc[:, sl], vn[:, sl]], axis=0)
        s = lax.dot_general(qg, kw, (((1,), (1,)), ((), ())), preferred_element_type=F32)
        s = s * scale - pen_ref[g]
        col = lax.broadcasted_iota(jnp.int32, s.shape, 1)
        outside = (is_first & (col < blk)) | (is_last & (col >= 2 * blk))
        s = jnp.where(outside, NEG_INF, s)
        sk = sink_ref[g]
        m = jnp.maximum(jnp.max(s, axis=-1, keepdims=True), sk)
        e = jnp.exp(s - m)
        den = jnp.sum(e, axis=-1, keepdims=True) + jnp.exp(sk - m)
        p = (e / den).astype(BF16)
        og = jnp.dot(p, vw, preferred_element_type=F32).astype(BF16)
        outs.extend([og[j * blk:(j + 1) * blk] for j in range(SWA_GROUP)])
    o_ref[...] = jnp.concatenate(outs, axis=-1)


def swa_attention(z, sink_logit, seq_lens):
    t = z.shape[0]
    blk = SWA_BLOCK
    assert all(s % blk == 0 for s in seq_lens)
    first_of, last_of = _edge_fns(seq_lens, blk)
    qcol, kcol, vcol = Z_SWA_Q // SWA_WIDTH, Z_SWA_K // SWA_KV_WIDTH, Z_SWA_V // SWA_KV_WIDTH
    assert qcol * SWA_WIDTH == Z_SWA_Q and kcol * SWA_KV_WIDTH == Z_SWA_K
    qi = np.arange(blk)[:, None]
    kj = np.arange(3 * blk)[None, :]
    rel = np.abs(kj - blk - qi).astype(np.float32)
    slopes = jnp.exp2(-jnp.arange(1, SWA_Q_HEADS + 1, dtype=F32) * (8.0 / SWA_Q_HEADS))
    pen = jnp.where(jnp.asarray(rel <= SWA_WINDOW)[None], slopes[:, None, None] * rel[None], -NEG_INF)
    pen = pen.reshape(SWA_KV_HEADS, SWA_GROUP * blk, 3 * blk)
    sink = jnp.repeat(sink_logit.astype(F32), blk).reshape(SWA_KV_HEADS, SWA_GROUP * blk, 1)
    prev = lambda i: jnp.maximum(i - 1, first_of(i))
    nxt = lambda i: jnp.minimum(i + 1, last_of(i))
    same = lambda i: i
    kv = lambda col, f: pl.BlockSpec((blk, SWA_KV_WIDTH), lambda i: (f(i), col))
    return pl.pallas_call(
        functools.partial(_swa_kernel, first_of, last_of),
        grid=(t // blk,),
        in_specs=[
            pl.BlockSpec((blk, SWA_WIDTH), lambda i: (i, qcol)),
            kv(kcol, prev), kv(kcol, same), kv(kcol, nxt),
            kv(vcol, prev), kv(vcol, same), kv(vcol, nxt),
            pl.BlockSpec(pen.shape, lambda i: (0, 0, 0)),
            pl.BlockSpec(sink.shape, lambda i: (0, 0, 0)),
        ],
        out_specs=pl.BlockSpec((blk, SWA_WIDTH), lambda i: (i, 0)),
        out_shape=jax.ShapeDtypeStruct((t, SWA_WIDTH), BF16),
        compiler_params=_params("parallel"),
        name="swa_attention",
    )(z, z, z, z, z, z, z, pen, sink)


NA_QROWS = 4
NA_QBLK = NA_QROWS * GRID_W
NA_PAIRS = 3 * NA_QROWS // 2


def _na_kernel(first_of, last_of, q_ref, kp_ref, kc_ref, kn_ref, vp_ref, vc_ref, vn_ref, bias_ref,
               rowmask_ref, o_ref):
    b = pl.program_id(0)
    first = first_of(b)
    n_rows = NA_QROWS * (last_of(b) - first + 1)
    q = q_ref[...]
    kp, kc, kn = kp_ref[...], kc_ref[...], kn_ref[...]
    vp, vc, vn = vp_ref[...], vc_ref[...], vn_ref[...]
    scale = HEAD_DIM ** -0.5
    codes = []
    for i in range(NA_QROWS):
        r = NA_QROWS * (b - first) + i
        r0 = jnp.clip(r - NA_ROWS // 2, 0, n_rows - NA_ROWS)
        v = r - r0
        in_window = lambda dr: ((dr >= NA_ROWS - 1 - v) & (dr < 2 * NA_ROWS - 1 - v)).astype(jnp.int32)
        codes.append([in_window(2 * p - i + NA_QROWS - 1) + 2 * in_window(2 * p - i + NA_QROWS)
                      for p in range(NA_PAIRS)])
    outs = []
    for h in range(NA_HEADS):
        sl = slice(h * HEAD_DIM, (h + 1) * HEAD_DIM)
        kw = jnp.concatenate([kp[:, sl], kc[:, sl], kn[:, sl]], axis=0)
        vw = jnp.concatenate([vp[:, sl], vc[:, sl], vn[:, sl]], axis=0)
        s = lax.dot_general(q[:, sl], kw, (((1,), (1,)), ((), ())), preferred_element_type=F32) * scale
        probs = []
        for i in range(NA_QROWS):
            tiles = []
            for p in range(NA_PAIRS):
                dr0 = 2 * p - i + NA_QROWS - 1
                t = s[i * GRID_W:(i + 1) * GRID_W, p * 2 * GRID_W:(p + 1) * 2 * GRID_W]
                tiles.append(t + bias_ref[h, dr0] + rowmask_ref[codes[i][p]])
            si = jnp.concatenate(tiles, axis=-1)
            e = jnp.exp(si - jnp.max(si, axis=-1, keepdims=True))
            probs.append((e / jnp.sum(e, axis=-1, keepdims=True)).astype(BF16))
        pm = jnp.concatenate(probs, axis=0)
        outs.append(jnp.dot(pm, vw, preferred_element_type=F32).astype(BF16))
    o_ref[...] = jnp.concatenate(outs, axis=-1)


def _na_tables(na_bias):
    c = np.arange(GRID_W)
    c0 = np.clip(c - NA_COLS // 2, 0, GRID_W - NA_COLS)
    in_col = (c[None, :] >= c0[:, None]) & (c[None, :] < c0[:, None] + NA_COLS)
    dc_idx = np.clip(c[None, :] - c[:, None], -(NA_COLS - 1), NA_COLS - 1) + NA_COLS - 1
    tab = na_bias.astype(F32)[:, :, dc_idx]
    tab = jnp.where(jnp.asarray(in_col)[None, None], tab, NEG_INF)
    pair = jnp.concatenate([tab[:, :-1], tab[:, 1:]], axis=-1)
    left = np.arange(2 * GRID_W) < GRID_W
    masks = [np.broadcast_to(np.where(np.where(left, code & 1, code >> 1) > 0, 0.0, NEG_INF), (GRID_W, 2 * GRID_W))
             for code in range(4)]
    return pair, jnp.asarray(np.stack(masks), F32)


def na_attention(z, na_bias, seq_lens):
    t = z.shape[0]
    assert all(s % NA_QBLK == 0 and s // GRID_W >= 2 * NA_ROWS for s in seq_lens)
    assert 2 * NA_QROWS == NA_ROWS
    first_of, last_of = _edge_fns(seq_lens, NA_QBLK)
    pair, rowmask = _na_tables(na_bias)
    prev = lambda i: jnp.maximum(i - 1, first_of(i))
    nxt = lambda i: jnp.minimum(i + 1, last_of(i))
    same = lambda i: i
    blk = lambda col, f: pl.BlockSpec((NA_QBLK, NA_WIDTH), lambda i: (f(i), col))
    qcol, kcol, vcol = Z_NA_Q // NA_WIDTH, Z_NA_K // NA_WIDTH, Z_NA_V // NA_WIDTH
    return pl.pallas_call(
        functools.partial(_na_kernel, first_of, last_of),
        grid=(t // NA_QBLK,),
        in_specs=[
            blk(qcol, same),
            blk(kcol, prev), blk(kcol, same), blk(kcol, nxt),
            blk(vcol, prev), blk(vcol, same), blk(vcol, nxt),
            pl.BlockSpec(pair.shape, lambda i: (0, 0, 0, 0)),
            pl.BlockSpec(rowmask.shape, lambda i: (0, 0, 0)),
        ],
        out_specs=pl.BlockSpec((NA_QBLK, NA_WIDTH), lambda i: (i, 0)),
        out_shape=jax.ShapeDtypeStruct((t, NA_WIDTH), BF16),
        compiler_params=_params("parallel"),
        name="na_attention",
    )(z, z, z, z, z, z, z, pair, rowmask)


FN_N2 = 128


def _dft_cos_sin(n):
    k = np.arange(n, dtype=np.float64)
    ang = 2.0 * np.pi * np.outer(k, k) / n
    return np.cos(ang), np.sin(ang)


def _fnet_a_kernel(f_ref, u_ref, y_ref):
    y_ref[...] = jnp.dot(f_ref[...], u_ref[...].astype(F32), preferred_element_type=F32, precision=HIGHEST)


def _fnet_bc_kernel(kb, yr_ref, yi_ref, twr_ref, twi_ref, m2_ref, mc_ref, fw_ref, o_ref):
    n2 = FN_N2
    for j in range(kb):
        yr, yi = yr_ref[j], yi_ref[j]
        twr = jnp.concatenate([twr_ref[j]] * FN_GROUPS, axis=-1)
        twi = jnp.concatenate([twi_ref[j]] * FN_GROUPS, axis=-1)
        z = jnp.concatenate([yr * twr - yi * twi, yr * twi + yi * twr], axis=0)
        w = jnp.dot(m2_ref[...], z, preferred_element_type=F32, precision=HIGHEST)
        outs = []
        for g in range(FN_GROUPS):
            sl = slice(g * FN_GROUP_DIM, (g + 1) * FN_GROUP_DIM)
            wg = jnp.concatenate([w[:n2, sl], w[n2:, sl]], axis=-1)
            fg = jnp.dot(wg, mc_ref[...], preferred_element_type=F32, precision=HIGHEST)
            outs.append(jnp.dot(fg.astype(BF16), fw_ref[g], preferred_element_type=F32).astype(BF16))
        o_ref[:, j * FN_WIDTH:(j + 1) * FN_WIDTH] = jnp.concatenate(outs, axis=-1)


def fnet_mix(u, fnet_w, kb=4, cw=4096):
    bsz, s, width = u.shape
    n2 = FN_N2
    n1 = s // n2
    assert n1 * n2 == s and n1 % kb == 0
    c1, s1 = _dft_cos_sin(n1)
    fa = jnp.asarray(np.concatenate([c1, -s1], axis=0), F32)
    cols = n2 * width
    cw = _row_tile(cols, cw)
    y = pl.pallas_call(
        _fnet_a_kernel,
        grid=(bsz, cols // cw),
        in_specs=[pl.BlockSpec((2 * n1, n1), lambda b, j: (0, 0)),
                  pl.BlockSpec((None, n1, cw), lambda b, j: (b, 0, j))],
        out_specs=pl.BlockSpec((None, 2 * n1, cw), lambda b, j: (b, 0, j)),
        out_shape=jax.ShapeDtypeStruct((bsz, 2 * n1, cols), F32),
        compiler_params=_params("parallel", "parallel"),
        name="fnet_stage_a",
    )(fa, u.reshape(bsz, n1, cols))
    y = y.reshape(bsz, 2, n1, n2, width)
    ang = 2.0 * np.pi * np.outer(np.arange(n1, dtype=np.float64), np.arange(n2, dtype=np.float64)) / s
    twr = jnp.asarray(np.broadcast_to(np.cos(ang)[..., None], (n1, n2, FN_GROUP_DIM)), F32)
    twi = jnp.asarray(np.broadcast_to(-np.sin(ang)[..., None], (n1, n2, FN_GROUP_DIM)), F32)
    c2, s2 = _dft_cos_sin(n2)
    m2 = jnp.asarray(np.block([[c2, s2], [-s2, c2]]), F32)
    cc, sc = _dft_cos_sin(FN_GROUP_DIM)
    mc = jnp.asarray(np.concatenate([cc, sc], axis=0) / math.sqrt(s * FN_GROUP_DIM), F32)
    out = pl.pallas_call(
        functools.partial(_fnet_bc_kernel, kb),
        grid=(bsz, n1 // kb),
        in_specs=[pl.BlockSpec((None, None, kb, n2, width), lambda b, k: (b, 0, k, 0, 0)),
                  pl.BlockSpec((None, None, kb, n2, width), lambda b, k: (b, 1, k, 0, 0)),
                  pl.BlockSpec((kb, n2, FN_GROUP_DIM), lambda b, k: (k, 0, 0)),
                  pl.BlockSpec((kb, n2, FN_GROUP_DIM), lambda b, k: (k, 0, 0)),
                  pl.BlockSpec(m2.shape, lambda b, k: (0, 0)),
                  pl.BlockSpec(mc.shape, lambda b, k: (0, 0)),
                  pl.BlockSpec(fnet_w.shape, lambda b, k: (0, 0, 0))],
        out_specs=pl.BlockSpec((None, n2, kb * width), lambda b, k: (b, 0, k)),
        out_shape=jax.ShapeDtypeStruct((bsz, n2, n1 * width), BF16),
        compiler_params=_params("parallel", "parallel"),
        name="fnet_stage_bc",
    )(y, y, twr, twi, m2, mc, fnet_w)
    return out.reshape(bsz, s, width)


def _post_mix_kernel(x_ref, a_ref, f_ref, c_ref, wout_ref, gq_ref, wq_ref, kv_ref, wo_ref, gf_ref, wr_ref,
                     x2_ref, h3_ref, aff_ref):
    x1 = x_ref[...]
    x1 = x1 + jnp.dot(a_ref[...], wout_ref[0:NA_WIDTH], preferred_element_type=F32)
    x1 = x1 + jnp.dot(f_ref[...], wout_ref[NA_WIDTH:NA_WIDTH + FN_WIDTH], preferred_element_type=F32)
    x1 = x1 + jnp.dot(c_ref[...], wout_ref[NA_WIDTH + FN_WIDTH:], preferred_element_type=F32)
    hq = _rms(x1, gq_ref[...]).astype(BF16)
    q = jnp.dot(hq, wq_ref[...], preferred_element_type=F32).astype(BF16)
    kv = kv_ref[...]
    scale = HEAD_DIM ** -0.5
    heads = []
    for h in range(MEM_HEADS):
        qh = q[:, h * HEAD_DIM:(h + 1) * HEAD_DIM]
        kh = kv[:, h * HEAD_DIM:(h + 1) * HEAD_DIM]
        vh = kv[:, MEM_WIDTH + h * HEAD_DIM:MEM_WIDTH + (h + 1) * HEAD_DIM]
        s = lax.dot_general(qh, kh, (((1,), (1,)), ((), ())), preferred_element_type=F32) * scale
        e = jnp.exp(s - jnp.max(s, axis=-1, keepdims=True))
        p = e / jnp.sum(e, axis=-1, keepdims=True)
        heads.append(jnp.dot(p.astype(BF16), vh, preferred_element_type=F32).astype(BF16))
    o = jnp.concatenate(heads, axis=-1)
    x2 = x1 + jnp.dot(o, wo_ref[...], preferred_element_type=F32)
    x2_ref[...] = x2
    h3 = _rms(x2, gf_ref[...])
    h3_ref[...] = h3
    logits = jnp.dot(h3.astype(BF16), wr_ref[...], preferred_element_type=F32)
    el = jnp.exp(logits - jnp.max(logits, axis=-1, keepdims=True))
    aff_ref[...] = el / jnp.sum(el, axis=-1, keepdims=True)


def post_mix(x, a, f, c, w_out, g_q, wq, kv, wo, g_f, w_router, seq_lens, tm=512):
    rows, d = x.shape
    n_mem = kv.shape[0] // len(seq_lens)
    n_exp = w_router.shape[1]
    tm = _row_tile(math.gcd(*seq_lens), tm)
    bounds = np.cumsum(seq_lens)[:-1] // tm

    def seq_of(i):
        s = jnp.int32(0)
        for b in bounds:
            s = s + (i >= int(b)).astype(jnp.int32)
        return s

    const = lambda i: (0, 0)
    row = lambda width: pl.BlockSpec((tm, width), lambda i: (i, 0))
    return pl.pallas_call(
        _post_mix_kernel,
        grid=(rows // tm,),
        in_specs=[
            row(d), row(a.shape[1]), row(f.shape[1]), row(c.shape[1]),
            pl.BlockSpec(w_out.shape, const),
            pl.BlockSpec((1, d), const),
            pl.BlockSpec(wq.shape, const),
            pl.BlockSpec((n_mem, kv.shape[1]), lambda i: (seq_of(i), 0)),
            pl.BlockSpec(wo.shape, const),
            pl.BlockSpec((1, d), const),
            pl.BlockSpec(w_router.shape, const),
        ],
        out_specs=[row(d), row(d), row(n_exp)],
        out_shape=[
            jax.ShapeDtypeStruct((rows, d), F32),
            jax.ShapeDtypeStruct((rows, d), F32),
            jax.ShapeDtypeStruct((rows, n_exp), F32),
        ],
        compiler_params=_params("parallel"),
        name="post_mix",
    )(x, a, f, c, w_out, g_q.reshape(1, d), wq, kv, wo, g_f.reshape(1, d), w_router)


LANES = 128


def _running_count(m, tri, ones, lower):
    n_exp, c, _ = m.shape
    mb = m.astype(BF16).reshape(n_exp * c, LANES)
    within = jnp.dot(mb, tri, preferred_element_type=F32).reshape(n_exp, c, LANES)
    rowsum = jnp.dot(mb, ones, preferred_element_type=F32).reshape(n_exp, c, LANES)
    before = jnp.stack([jnp.dot(lower, rowsum[e].astype(BF16), preferred_element_type=F32)
                        for e in range(n_exp)], axis=0)
    return within - m + before


def _select_kernel(cap, a_ref, pos_ref, sel_ref):
    a = a_ref[...]
    n_exp, c, _ = a.shape
    bits = pltpu.bitcast(a, jnp.int32)
    count = lambda mask: jnp.sum(jnp.sum(mask.astype(F32), axis=1, keepdims=True), axis=2, keepdims=True)
    thr = jnp.zeros((n_exp, 1, 1), jnp.int32)
    for b in range(30, -1, -1):
        trial = thr | (1 << b)
        thr = jnp.where(count(bits >= trial) >= cap, trial, thr)
    above = bits > thr
    tied = bits == thr
    n_tied_taken = cap - count(above)
    r = lax.broadcasted_iota(jnp.int32, (LANES, LANES), 0)
    cidx = lax.broadcasted_iota(jnp.int32, (LANES, LANES), 1)
    tri = (r <= cidx).astype(F32).astype(BF16)
    ones = jnp.ones((LANES, LANES), BF16)
    lower = (lax.broadcasted_iota(jnp.int32, (c, c), 1) < lax.broadcasted_iota(jnp.int32, (c, c), 0))
    lower = lower.astype(F32).astype(BF16)
    tie_rank = _running_count(tied.astype(F32), tri, ones, lower)
    sel = (above | (tied & (tie_rank < n_tied_taken))).astype(F32)
    pos_ref[...] = _running_count(sel, tri, ones, lower).astype(jnp.int32)
    sel_ref[...] = sel.astype(jnp.int32)


def route_select(aff_t, cap):
    n_exp, n = aff_t.shape
    assert n % (8 * LANES) == 0
    c = n // LANES
    out = jax.ShapeDtypeStruct((n_exp, c, LANES), jnp.int32)
    pos, sel = pl.pallas_call(
        functools.partial(_select_kernel, cap),
        out_shape=[out, out],
        compiler_params=pltpu.CompilerParams(vmem_limit_bytes=V7X_VMEM_LIMIT_BYTES),
        name="route_select",
    )(aff_t.reshape(n_exp, c, LANES))
    return pos.reshape(n_exp, n), sel.reshape(n_exp, n)


GR_TM = 256


def _gather_kernel(gt, idx_ref, idx_next_ref, h_hbm, o_ref, buf, sems):
    i = pl.program_id(0)
    slot = i % 2

    def row_copy(tok, k, s):
        return pltpu.make_async_copy(h_hbm.at[pl.ds(tok, 1)], buf.at[s, pl.ds(k, 1)], sems.at[s])

    def start_tile(ref, s):
        def body(k, carry):
            row_copy(ref[0, 0, k], k, s).start()
            return carry
        lax.fori_loop(0, gt, body, 0, unroll=8)

    @pl.when(i == 0)
    def _():
        start_tile(idx_ref, slot)

    @pl.when(i + 1 < pl.num_programs(0))
    def _():
        start_tile(idx_next_ref, 1 - slot)

    pltpu.make_async_copy(h_hbm.at[pl.ds(0, gt)], buf.at[slot], sems.at[slot]).wait()
    o_ref[...] = buf[slot].astype(BF16)


def gather_rows(h, idx, gt=GR_TM):
    t, d = h.shape
    r = idx.shape[0]
    assert r % gt == 0
    n = r // gt
    idx3 = idx.reshape(n, 1, gt)
    smem = lambda f: pl.BlockSpec((1, 1, gt), f, memory_space=pltpu.SMEM)
    return pl.pallas_call(
        functools.partial(_gather_kernel, gt),
        grid=(n,),
        in_specs=[smem(lambda i: (i, 0, 0)), smem(lambda i: (jnp.minimum(i + 1, n - 1), 0, 0)),
                  pl.BlockSpec(memory_space=pl.ANY)],
        out_specs=pl.BlockSpec((gt, d), lambda i: (i, 0)),
        out_shape=jax.ShapeDtypeStruct((r, d), BF16),
        scratch_shapes=[pltpu.VMEM((2, gt, d), F32), pltpu.SemaphoreType.DMA((2,))],
        compiler_params=_params("arbitrary"),
        name="gather_rows",
    )(idx3, idx3, h)


def _expert_ffn_kernel(n_hidden_steps, fk, xe_ref, gate_ref, wg_ref, wu_ref, wd_ref, y_ref, hid_ref):
    s = pl.program_id(2)

    @pl.when(s < n_hidden_steps)
    def _():
        xe = xe_ref[0]
        a = jnp.dot(xe, wg_ref[0].astype(BF16), preferred_element_type=F32)
        b = jnp.dot(xe, wu_ref[0].astype(BF16), preferred_element_type=F32)
        hid_ref[:, pl.ds(pl.multiple_of(s * fk, fk), fk)] = (a * jax.nn.sigmoid(a) * b).astype(BF16)

    @pl.when(s >= n_hidden_steps)
    def _():
        y = jnp.dot(hid_ref[...], wd_ref[0].astype(BF16), preferred_element_type=F32)
        y_ref[0] = (y * gate_ref[0]).astype(y_ref.dtype)


def expert_ffn(xe, gates, w_gate, w_up, w_down, layer, tm=1024, fk=512, nk=256):
    n_exp, rows, d = xe.shape
    d_ff = w_gate.shape[-1]
    tm = _row_tile(rows, tm)
    fk = _row_tile(d_ff, fk)
    nk = _row_tile(d, nk)
    nf, nn = d_ff // fk, d // nk
    hid_step = lambda s: jnp.minimum(s, nf - 1)
    out_step = lambda s: jnp.maximum(s - nf, 0)
    return pl.pallas_call(
        functools.partial(_expert_ffn_kernel, nf, fk),
        grid=(n_exp, rows // tm, nf + nn),
        in_specs=[
            pl.BlockSpec((1, tm, d), lambda e, m, s: (e, m, 0)),
            pl.BlockSpec((1, tm, 1), lambda e, m, s: (e, m, 0)),
            pl.BlockSpec((None, 1, d, fk), lambda e, m, s: (layer, e, 0, hid_step(s))),
            pl.BlockSpec((None, 1, d, fk), lambda e, m, s: (layer, e, 0, hid_step(s))),
            pl.BlockSpec((None, 1, d_ff, nk), lambda e, m, s: (layer, e, 0, out_step(s))),
        ],
        out_specs=pl.BlockSpec((1, tm, nk), lambda e, m, s: (e, m, out_step(s))),
        out_shape=jax.ShapeDtypeStruct((n_exp, rows, d), BF16),
        scratch_shapes=[pltpu.VMEM((tm, d_ff), BF16)],
        compiler_params=_params("parallel", "parallel", "arbitrary"),
        name="expert_ffn",
    )(xe, gates, w_gate, w_up, w_down)


CB_TM = 256
CB_CH = 32
CB_KB = 256


def _combine_kernel(n_exp, tm, p0_ref, x2_ref, y_hbm, tok_hbm, o_ref, ybuf, tokbuf, ysem, toksem):
    i = pl.program_id(0)
    ch, kb = CB_CH, CB_KB

    @pl.when(i == 0)
    def _():
        ybuf[...] = jnp.zeros_like(ybuf)
        tokbuf[...] = jnp.full(tokbuf.shape, -1, jnp.int32)

    def y_copy(e, src, dst):
        return pltpu.make_async_copy(y_hbm.at[e, pl.ds(src, ch)], ybuf.at[pl.ds(dst, ch)], ysem)

    def tok_copy(e, src, dst):
        return pltpu.make_async_copy(tok_hbm.at[e, pl.ds(src, ch)], tokbuf.at[pl.ds(dst, ch)], toksem)

    off = jnp.int32(0)
    for e in range(n_exp):
        p0 = p0_ref[i, e]
        p1 = p0_ref[i + 1, e]
        a0 = (p0 // ch) * ch
        nch = jnp.where(p1 > p0, (p1 - a0 + ch - 1) // ch, 0)

        def issue(k, carry, e=e, a0=a0, off=off):
            src = pl.multiple_of(a0 + k * ch, ch)
            dst = pl.multiple_of(off + k * ch, ch)
            y_copy(e, src, dst).start()
            tok_copy(e, src, dst).start()
            return carry

        lax.fori_loop(0, nch, issue, 0)
        off = off + nch * ch

    def drain(k, carry):
        y_copy(0, 0, 0).wait()
        tok_copy(0, 0, 0).wait()
        return carry

    lax.fori_loop(0, off // ch, drain, 0)

    o_ref[...] = x2_ref[...]
    t0 = i * tm
    lane = lax.broadcasted_iota(jnp.int32, (kb, 128), 1)
    row = lax.broadcasted_iota(jnp.int32, (kb, 128), 0)

    def block(j, carry):
        r0 = pl.multiple_of(j * kb, kb)
        tok = jnp.where(row + r0 < off, tokbuf[pl.ds(r0, kb), :] - t0, -1)
        bt = jnp.concatenate([(tok == lane + 128 * c) for c in range(tm // 128)], axis=-1)
        bt = bt.astype(F32).astype(BF16)
        o_ref[...] += lax.dot_general(bt, ybuf[pl.ds(r0, kb), :], (((0,), (0,)), ((), ())),
                                      preferred_element_type=F32)
        return carry

    lax.fori_loop(0, (off + kb - 1) // kb, block, 0)


def moe_combine(x2, y, idx_sorted, p0, tm=CB_TM):
    t, d = x2.shape
    n_exp, r, _ = y.shape
    assert t % tm == 0 and r % CB_CH == 0 and tm % 128 == 0
    n_tiles = t // tm
    assert p0.shape == (n_tiles + 1, n_exp)
    tok = jnp.broadcast_to(idx_sorted[..., None], (n_exp, r, 128))
    buf_rows = pl.cdiv(n_exp * (tm + 2 * CB_CH), CB_KB) * CB_KB
    grid_spec = pltpu.PrefetchScalarGridSpec(
        num_scalar_prefetch=1,
        grid=(n_tiles,),
        in_specs=[pl.BlockSpec((tm, d), lambda i, p: (i, 0)),
                  pl.BlockSpec(memory_space=pl.ANY),
                  pl.BlockSpec(memory_space=pl.ANY)],
        out_specs=pl.BlockSpec((tm, d), lambda i, p: (i, 0)),
        scratch_shapes=[pltpu.VMEM((buf_rows, d), BF16), pltpu.VMEM((buf_rows, 128), jnp.int32),
                        pltpu.SemaphoreType.DMA(()), pltpu.SemaphoreType.DMA(())],
    )
    return pl.pallas_call(
        functools.partial(_combine_kernel, n_exp, tm),
        grid_spec=grid_spec,
        out_shape=jax.ShapeDtypeStruct((t, d), F32),
        compiler_params=_params("arbitrary"),
        name="moe_combine",
    )(p0, x2, y, tok)


def _final_norm_kernel(x_ref, g_ref, o_ref):
    o_ref[...] = _rms(x_ref[...], g_ref[...])


def final_norm(x, g, row0, rows, tm=1024):
    d = x.shape[1]
    tm = _row_tile(math.gcd(rows, row0) if row0 else rows, tm)
    blk0 = row0 // tm
    return pl.pallas_call(
        _final_norm_kernel,
        grid=(rows // tm,),
        in_specs=[pl.BlockSpec((tm, d), lambda i: (i + blk0, 0)), pl.BlockSpec((1, d), lambda i: (0, 0))],
        out_specs=pl.BlockSpec((tm, d), lambda i: (i, 0)),
        out_shape=jax.ShapeDtypeStruct((rows, d), F32),
        compiler_params=_params("parallel"),
        name="final_norm",
    )(x, g.reshape(1, d))


def kernel(x_prompt, x_sample, mem_prompt, mem_sample, w_in, w_out, na_bias, fnet_w, sink_logit,
           wq_mem, wkv_mem, wo_mem, w_router, w_gate, w_up, w_down,
           g_mix, g_mem_q, g_mem_kv, g_ffn, g_final):
    d = x_prompt.shape[-1]
    depth = w_in.shape[0]
    n_exp = w_router.shape[-1]
    groups = [(x_prompt.shape[0], x_prompt.shape[1]), (x_sample.shape[0], x_sample.shape[1])]
    seq_lens = [s for b, s in groups for _ in range(b)]
    group_tokens = [b * s for b, s in groups]
    caps = [EC_CAPACITY_FACTOR * n // n_exp for n in group_tokens]
    offs = np.concatenate([[0], np.cumsum(group_tokens)]).tolist()

    x = jnp.concatenate([x_prompt.reshape(-1, d), x_sample.reshape(-1, d)], axis=0)
    mem = jnp.concatenate([mem_prompt.reshape(-1, d), mem_sample.reshape(-1, d)], axis=0)

    w_in_b, w_out_b = w_in.astype(BF16), w_out.astype(BF16)
    wq_b, wkv_b, wo_b = wq_mem.astype(BF16), wkv_mem.astype(BF16), wo_mem.astype(BF16)
    wr_b, fnet_b = w_router.astype(BF16), fnet_w.astype(BF16)

    for l in range(depth):
        z = norm_matmul(x, g_mix[l], w_in_b[l], BF16)
        a = na_attention(z, na_bias[l], seq_lens)
        c = swa_attention(z, sink_logit[l], seq_lens)
        u = z[:, Z_FN:Z_FN + FN_WIDTH]
        f = jnp.concatenate(
            [fnet_mix(u[offs[gi]:offs[gi + 1]].reshape(b, s, FN_WIDTH), fnet_b[l]).reshape(b * s, FN_WIDTH)
             for gi, (b, s) in enumerate(groups)], axis=0)
        kv = norm_matmul(mem, g_mem_kv[l], wkv_b[l], BF16)
        x2, h3, aff = post_mix(x, a, f, c, w_out_b[l], g_mem_q[l], wq_b[l], kv, wo_b[l], g_ffn[l],
                               wr_b[l], seq_lens)
        aff_t = aff.T
        idxs, p0s, slot0 = [], [], 0
        for gi in range(len(groups)):
            pos, sel = route_select(aff_t[:, offs[gi]:offs[gi + 1]], caps[gi])
            ranks = jnp.arange(1, caps[gi] + 1, dtype=jnp.int32)
            idxs.append(jax.vmap(lambda row: jnp.searchsorted(row, ranks, side='left'))(pos + sel).astype(jnp.int32)
                        + offs[gi])
            p0s.append(pos[:, ::CB_TM].T + slot0)
            slot0 += caps[gi]
        idx = jnp.concatenate(idxs, axis=1)
        p0 = jnp.concatenate(p0s + [jnp.full((1, n_exp), slot0, jnp.int32)], axis=0)
        gate = jnp.take_along_axis(aff_t, idx, axis=1)[..., None]
        xe = gather_rows(h3, idx.reshape(-1)).reshape(n_exp, slot0, d)
        y = expert_ffn(xe, gate, w_gate, w_up, w_down, l)
        x = moe_combine(x2, y, idx, p0)

    y_prompt = final_norm(x, g_final, offs[0], group_tokens[0]).reshape(x_prompt.shape)
    y_sample = final_norm(x, g_final, offs[1], group_tokens[1]).reshape(x_sample.shape)
    return (y_prompt, y_sample)
```

```python
import functools
import math

import jax
import jax.numpy as jnp
import numpy as np
from jax import lax
from jax.experimental import pallas as pl
from jax.experimental.pallas import tpu as pltpu

HEAD_DIM = 128
NA_HEADS = 4
NA_WIDTH = NA_HEADS * HEAD_DIM
FN_GROUPS = 4
FN_GROUP_DIM = 128
FN_WIDTH = FN_GROUPS * FN_GROUP_DIM
SWA_Q_HEADS = 8
SWA_KV_HEADS = 2
SWA_GROUP = SWA_Q_HEADS // SWA_KV_HEADS
SWA_WIDTH = SWA_Q_HEADS * HEAD_DIM
SWA_KV_WIDTH = SWA_KV_HEADS * HEAD_DIM
GRID_W = 64
NA_ROWS = 8
NA_COLS = 16
SWA_WINDOW = 128
SWA_BLOCK = 128
MEM_HEADS = 4
MEM_WIDTH = MEM_HEADS * HEAD_DIM
EC_CAPACITY_FACTOR = 2
RMS_EPS = 1e-6
NEG_INF = -1e30

Z_NA_Q, Z_NA_K, Z_NA_V = 0, NA_WIDTH, 2 * NA_WIDTH
Z_FN = 3 * NA_WIDTH
Z_SWA_Q = Z_FN + FN_WIDTH
Z_SWA_K = Z_SWA_Q + SWA_WIDTH
Z_SWA_V = Z_SWA_K + SWA_KV_WIDTH

V7X_VMEM_LIMIT_BYTES = 56 * 1024 * 1024

BF16 = jnp.bfloat16
F32 = jnp.float32
HIGHEST = lax.Precision.HIGHEST


def _params(*semantics):
    return pltpu.CompilerParams(dimension_semantics=semantics, vmem_limit_bytes=V7X_VMEM_LIMIT_BYTES)


def _rms(x, g):
    ms = jnp.mean(x * x, axis=-1, keepdims=True)
    return x * lax.rsqrt(ms + RMS_EPS) * g


def _row_tile(rows, want):
    t = min(rows, want)
    while rows % t:
        t //= 2
    return t


def _edge_fns(seq_lens, blk):
    firsts, lasts = [], []
    o = 0
    for s in seq_lens:
        firsts.append(o // blk)
        o += s
        lasts.append(o // blk - 1)

    def first_of(i):
        f = jnp.int32(firsts[0])
        for b in firsts[1:]:
            f = jnp.where(i >= b, jnp.int32(b), f)
        return f

    def last_of(i):
        l = jnp.int32(lasts[-1])
        for b in reversed(lasts[:-1]):
            l = jnp.where(i <= b, jnp.int32(b), l)
        return l

    return first_of, last_of


def _norm_matmul_kernel(x_ref, g_ref, w_ref, o_ref):
    h = _rms(x_ref[...], g_ref[...]).astype(BF16)
    o_ref[...] = jnp.dot(h, w_ref[...], preferred_element_type=F32).astype(o_ref.dtype)


def norm_matmul(x, g, w, out_dtype, tm=512):
    rows, d = x.shape
    n = w.shape[1]
    tm = _row_tile(rows, tm)
    return pl.pallas_call(
        _norm_matmul_kernel,
        grid=(rows // tm,),
        in_specs=[
            pl.BlockSpec((tm, d), lambda i: (i, 0)),
            pl.BlockSpec((1, d), lambda i: (0, 0)),
            pl.BlockSpec((d, n), lambda i: (0, 0)),
        ],
        out_specs=pl.BlockSpec((tm, n), lambda i: (i, 0)),
        out_shape=jax.ShapeDtypeStruct((rows, n), out_dtype),
        compiler_params=_params("parallel"),
        name="norm_matmul",
    )(x, g.reshape(1, d), w)


def _swa_kernel(first_of, last_of, q_ref, kp_ref, kc_ref, kn_ref, vp_ref, vc_ref, vn_ref, pen_ref,
                sink_ref, o_ref):
    i = pl.program_id(0)
    is_first = i == first_of(i)
    is_last = i == last_of(i)
    q = q_ref[...]
    kp, kc, kn = kp_ref[...], kc_ref[...], kn_ref[...]
    vp, vc, vn = vp_ref[...], vc_ref[...], vn_ref[...]
    scale = HEAD_DIM ** -0.5
    blk = SWA_BLOCK
    outs = []
    for g in range(SWA_KV_HEADS):
        sl = slice(g * HEAD_DIM, (g + 1) * HEAD_DIM)
        qg = jnp.concatenate([q[:, (g * SWA_GROUP + j) * HEAD_DIM:(g * SWA_GROUP + j + 1) * HEAD_DIM]
                              for j in range(SWA_GROUP)], axis=0)
        kw = jnp.concatenate([kp[:, sl], kc[:, sl], kn[:, sl]], axis=0)
        vw = jnp.concatenate([vp[:, sl], vc[:, sl], vn[:, sl]], axis=0)
        s = lax.dot_general(qg, kw, (((1,), (1,)), ((), ())), preferred_element_type=F32)
        s = s * scale - pen_ref[g]
        col = lax.broadcasted_iota(jnp.int32, s.shape, 1)
        outside = (is_first & (col < blk)) | (is_last & (col >= 2 * blk))
        s = jnp.where(outside, NEG_INF, s)
        sk = sink_ref[g]
        m = jnp.maximum(jnp.max(s, axis=-1, keepdims=True), sk)
        e = jnp.exp(s - m)
        den = jnp.sum(e, axis=-1, keepdims=True) + jnp.exp(sk - m)
        p = (e / den).astype(BF16)
        og = jnp.dot(p, vw, preferred_element_type=F32).astype(BF16)
        outs.extend([og[j * blk:(j + 1) * blk] for j in range(SWA_GROUP)])
    o_ref[...] = jnp.concatenate(outs, axis=-1)


def swa_attention(z, sink_logit, seq_lens):
    t = z.shape[0]
    blk = SWA_BLOCK
    assert all(s % blk == 0 for s in seq_lens)
    first_of, last_of = _edge_fns(seq_lens, blk)
    qcol, kcol, vcol = Z_SWA_Q // SWA_WIDTH, Z_SWA_K // SWA_KV_WIDTH, Z_SWA_V // SWA_KV_WIDTH
    assert qcol * SWA_WIDTH == Z_SWA_Q and kcol * SWA_KV_WIDTH == Z_SWA_K
    qi = np.arange(blk)[:, None]
    kj = np.arange(3 * blk)[None, :]
    rel = np.abs(kj - blk - qi).astype(np.float32)
    slopes = jnp.exp2(-jnp.arange(1, SWA_Q_HEADS + 1, dtype=F32) * (8.0 / SWA_Q_HEADS))
    pen = jnp.where(jnp.asarray(rel <= SWA_WINDOW)[None], slopes[:, None, None] * rel[None], -NEG_INF)
    pen = pen.reshape(SWA_KV_HEADS, SWA_GROUP * blk, 3 * blk)
    sink = jnp.repeat(sink_logit.astype(F32), blk).reshape(SWA_KV_HEADS, SWA_GROUP * blk, 1)
    prev = lambda i: jnp.maximum(i - 1, first_of(i))
    nxt = lambda i: jnp.minimum(i + 1, last_of(i))
    same = lambda i: i
    kv = lambda col, f: pl.BlockSpec((blk, SWA_KV_WIDTH), lambda i: (f(i), col))
    return pl.pallas_call(
        functools.partial(_swa_kernel, first_of, last_of),
        grid=(t // blk,),
        in_specs=[
            pl.BlockSpec((blk, SWA_WIDTH), lambda i: (i, qcol)),
            kv(kcol, prev), kv(kcol, same), kv(kcol, nxt),
            kv(vcol, prev), kv(vcol, same), kv(vcol, nxt),
            pl.BlockSpec(pen.shape, lambda i: (0, 0, 0)),
            pl.BlockSpec(sink.shape, lambda i: (0, 0, 0)),
        ],
        out_specs=pl.BlockSpec((blk, SWA_WIDTH), lambda i: (i, 0)),
        out_shape=jax.ShapeDtypeStruct((t, SWA_WIDTH), BF16),
        compiler_params=_params("parallel"),
        name="swa_attention",
    )(z, z, z, z, z, z, z, pen, sink)


NA_QROWS = 4
NA_QBLK = NA_QROWS * GRID_W
NA_PAIRS = 3 * NA_QROWS // 2


def _na_kernel(first_of, last_of, q_ref, kp_ref, kc_ref, kn_ref, vp_ref, vc_ref, vn_ref, bias_ref,
               rowmask_ref, o_ref):
    b = pl.program_id(0)
    first = first_of(b)
    n_rows = NA_QROWS * (last_of(b) - first + 1)
    q = q_ref[...]
    kp, kc, kn = kp_ref[...], kc_ref[...], kn_ref[...]
    vp, vc, vn = vp_ref[...], vc_ref[...], vn_ref[...]
    scale = HEAD_DIM ** -0.5
    codes = []
    for i in range(NA_QROWS):
        r = NA_QROWS * (b - first) + i
        r0 = jnp.clip(r - NA_ROWS // 2, 0, n_rows - NA_ROWS)
        v = r - r0
        in_window = lambda dr: ((dr >= NA_ROWS - 1 - v) & (dr < 2 * NA_ROWS - 1 - v)).astype(jnp.int32)
        codes.append([in_window(2 * p - i + NA_QROWS - 1) + 2 * in_window(2 * p - i + NA_QROWS)
                      for p in range(NA_PAIRS)])
    outs = []
    for h in range(NA_HEADS):
        sl = slice(h * HEAD_DIM, (h + 1) * HEAD_DIM)
        kw = jnp.concatenate([kp[:, sl], kc[:, sl], kn[:, sl]], axis=0)
        vw = jnp.concatenate([vp[:, sl], vc[:, sl], vn[:, sl]], axis=0)
        s = lax.dot_general(q[:, sl], kw, (((1,), (1,)), ((), ())), preferred_element_type=F32) * scale
        probs = []
        for i in range(NA_QROWS):
            tiles = []
            for p in range(NA_PAIRS):
                dr0 = 2 * p - i + NA_QROWS - 1
                t = s[i * GRID_W:(i + 1) * GRID_W, p * 2 * GRID_W:(p + 1) * 2 * GRID_W]
                tiles.append(t + bias_ref[h, dr0] + rowmask_ref[codes[i][p]])
            si = jnp.concatenate(tiles, axis=-1)
            e = jnp.exp(si - jnp.max(si, axis=-1, keepdims=True))
            probs.append((e / jnp.sum(e, axis=-1, keepdims=True)).astype(BF16))
        pm = jnp.concatenate(probs, axis=0)
        outs.append(jnp.dot(pm, vw, preferred_element_type=F32).astype(BF16))
    o_ref[...] = jnp.concatenate(outs, axis=-1)


def _na_tables(na_bias):
    c = np.arange(GRID_W)
    c0 = np.clip(c - NA_COLS // 2, 0, GRID_W - NA_COLS)
    in_col = (c[None, :] >= c0[:, None]) & (c[None, :] < c0[:, None] + NA_COLS)
    dc_idx = np.clip(c[None, :] - c[:, None], -(NA_COLS - 1), NA_COLS - 1) + NA_COLS - 1
    tab = na_bias.astype(F32)[:, :, dc_idx]
    tab = jnp.where(jnp.asarray(in_col)[None, None], tab, NEG_INF)
    pair = jnp.concatenate([tab[:, :-1], tab[:, 1:]], axis=-1)
    left = np.arange(2 * GRID_W) < GRID_W
    masks = [np.broadcast_to(np.where(np.where(left, code & 1, code >> 1) > 0, 0.0, NEG_INF), (GRID_W, 2 * GRID_W))
             for code in range(4)]
    return pair, jnp.asarray(np.stack(masks), F32)


def na_attention(z, na_bias, seq_lens):
    t = z.shape[0]
    assert all(s % NA_QBLK == 0 and s // GRID_W >= 2 * NA_ROWS for s in seq_lens)
    assert 2 * NA_QROWS == NA_ROWS
    first_of, last_of = _edge_fns(seq_lens, NA_QBLK)
    pair, rowmask = _na_tables(na_bias)
    prev = lambda i: jnp.maximum(i - 1, first_of(i))
    nxt = lambda i: jnp.minimum(i + 1, last_of(i))
    same = lambda i: i
    blk = lambda col, f: pl.BlockSpec((NA_QBLK, NA_WIDTH), lambda i: (f(i), col))
    qcol, kcol, vcol = Z_NA_Q // NA_WIDTH, Z_NA_K // NA_WIDTH, Z_NA_V // NA_WIDTH
    return pl.pallas_call(
        functools.partial(_na_kernel, first_of, last_of),
        grid=(t // NA_QBLK,),
        in_specs=[
            blk(qcol, same),
            blk(kcol, prev), blk(kcol, same), blk(kcol, nxt),
            blk(vcol, prev), blk(vcol, same), blk(vcol, nxt),
            pl.BlockSpec(pair.shape, lambda i: (0, 0, 0, 0)),
            pl.BlockSpec(rowmask.shape, lambda i: (0, 0, 0)),
        ],
        out_specs=pl.BlockSpec((NA_QBLK, NA_WIDTH), lambda i: (i, 0)),
        out_shape=jax.ShapeDtypeStruct((t, NA_WIDTH), BF16),
        compiler_params=_params("parallel"),
        name="na_attention",
    )(z, z, z, z, z, z, z, pair, rowmask)


FN_N2 = 128


def _dft_cos_sin(n):
    k = np.arange(n, dtype=np.float64)
    ang = 2.0 * np.pi * np.outer(k, k) / n
    return np.cos(ang), np.sin(ang)


def _fnet_a_kernel(f_ref, u_ref, y_ref):
    y_ref[...] = jnp.dot(f_ref[...], u_ref[...].astype(F32), preferred_element_type=F32, precision=HIGHEST)


def _fnet_bc_kernel(kb, yr_ref, yi_ref, twr_ref, twi_ref, m2_ref, mc_ref, fw_ref, o_ref):
    n2 = FN_N2
    for j in range(kb):
        yr, yi = yr_ref[j], yi_ref[j]
        twr = jnp.concatenate([twr_ref[j]] * FN_GROUPS, axis=-1)
        twi = jnp.concatenate([twi_ref[j]] * FN_GROUPS, axis=-1)
        z = jnp.concatenate([yr * twr - yi * twi, yr * twi + yi * twr], axis=0)
        w = jnp.dot(m2_ref[...], z, preferred_element_type=F32, precision=HIGHEST)
        outs = []
        for g in range(FN_GROUPS):
            sl = slice(g * FN_GROUP_DIM, (g + 1) * FN_GROUP_DIM)
            wg = jnp.concatenate([w[:n2, sl], w[n2:, sl]], axis=-1)
            fg = jnp.dot(wg, mc_ref[...], preferred_element_type=F32, precision=HIGHEST)
            outs.append(jnp.dot(fg.astype(BF16), fw_ref[g], preferred_element_type=F32).astype(BF16))
        o_ref[:, j * FN_WIDTH:(j + 1) * FN_WIDTH] = jnp.concatenate(outs, axis=-1)


def fnet_mix(u, fnet_w, kb=4, cw=4096):
    bsz, s, width = u.shape
    n2 = FN_N2
    n1 = s // n2
    assert n1 * n2 == s and n1 % kb == 0
    c1, s1 = _dft_cos_sin(n1)
    fa = jnp.asarray(np.concatenate([c1, -s1], axis=0), F32)
    cols = n2 * width
    cw = _row_tile(cols, cw)
    y = pl.pallas_call(
        _fnet_a_kernel,
        grid=(bsz, cols // cw),
        in_specs=[pl.BlockSpec((2 * n1, n1), lambda b, j: (0, 0)),
                  pl.BlockSpec((None, n1, cw), lambda b, j: (b, 0, j))],
        out_specs=pl.BlockSpec((None, 2 * n1, cw), lambda b, j: (b, 0, j)),
        out_shape=jax.ShapeDtypeStruct((bsz, 2 * n1, cols), F32),
        compiler_params=_params("parallel", "parallel"),
        name="fnet_stage_a",
    )(fa, u.reshape(bsz, n1, cols))
    y = y.reshape(bsz, 2, n1, n2, width)
    ang = 2.0 * np.pi * np.outer(np.arange(n1, dtype=np.float64), np.arange(n2, dtype=np.float64)) / s
    twr = jnp.asarray(np.broadcast_to(np.cos(ang)[..., None], (n1, n2, FN_GROUP_DIM)), F32)
    twi = jnp.asarray(np.broadcast_to(-np.sin(ang)[..., None], (n1, n2, FN_GROUP_DIM)), F32)
    c2, s2 = _dft_cos_sin(n2)
    m2 = jnp.asarray(np.block([[c2, s2], [-s2, c2]]), F32)
    cc, sc = _dft_cos_sin(FN_GROUP_DIM)
    mc = jnp.asarray(np.concatenate([cc, sc], axis=0) / math.sqrt(s * FN_GROUP_DIM), F32)
    out = pl.pallas_call(
        functools.partial(_fnet_bc_kernel, kb),
        grid=(bsz, n1 // kb),
        in_specs=[pl.BlockSpec((None, None, kb, n2, width), lambda b, k: (b, 0, k, 0, 0)),
                  pl.BlockSpec((None, None, kb, n2, width), lambda b, k: (b, 1, k, 0, 0)),
                  pl.BlockSpec((kb, n2, FN_GROUP_DIM), lambda b, k: (k, 0, 0)),
                  pl.BlockSpec((kb, n2, FN_GROUP_DIM), lambda b, k: (k, 0, 0)),
                  pl.BlockSpec(m2.shape, lambda b, k: (0, 0)),
                  pl.BlockSpec(mc.shape, lambda b, k: (0, 0)),
                  pl.BlockSpec(fnet_w.shape, lambda b, k: (0, 0, 0))],
        out_specs=pl.BlockSpec((None, n2, kb * width), lambda b, k: (b, 0, k)),
        out_shape=jax.ShapeDtypeStruct((bsz, n2, n1 * width), BF16),
        compiler_params=_params("parallel", "parallel"),
        name="fnet_stage_bc",
    )(y, y, twr, twi, m2, mc, fnet_w)
    return out.reshape(bsz, s, width)


def _post_mix_kernel(x_ref, a_ref, f_ref, c_ref, wout_ref, gq_ref, wq_ref, kv_ref, wo_ref, gf_ref, wr_ref,
                     x2_ref, h3_ref, aff_ref):
    x1 = x_ref[...]
    x1 = x1 + jnp.dot(a_ref[...], wout_ref[0:NA_WIDTH], preferred_element_type=F32)
    x1 = x1 + jnp.dot(f_ref[...], wout_ref[NA_WIDTH:NA_WIDTH + FN_WIDTH], preferred_element_type=F32)
    x1 = x1 + jnp.dot(c_ref[...], wout_ref[NA_WIDTH + FN_WIDTH:], preferred_element_type=F32)
    hq = _rms(x1, gq_ref[...]).astype(BF16)
    q = jnp.dot(hq, wq_ref[...], preferred_element_type=F32).astype(BF16)
    kv = kv_ref[...]
    scale = HEAD_DIM ** -0.5
    heads = []
    for h in range(MEM_HEADS):
        qh = q[:, h * HEAD_DIM:(h + 1) * HEAD_DIM]
        kh = kv[:, h * HEAD_DIM:(h + 1) * HEAD_DIM]
        vh = kv[:, MEM_WIDTH + h * HEAD_DIM:MEM_WIDTH + (h + 1) * HEAD_DIM]
        s = lax.dot_general(qh, kh, (((1,), (1,)), ((), ())), preferred_element_type=F32) * scale
        e = jnp.exp(s - jnp.max(s, axis=-1, keepdims=True))
        p = e / jnp.sum(e, axis=-1, keepdims=True)
        heads.append(jnp.dot(p.astype(BF16), vh, preferred_element_type=F32).astype(BF16))
    o = jnp.concatenate(heads, axis=-1)
    x2 = x1 + jnp.dot(o, wo_ref[...], preferred_element_type=F32)
    x2_ref[...] = x2
    h3 = _rms(x2, gf_ref[...])
    h3_ref[...] = h3
    logits = jnp.dot(h3.astype(BF16), wr_ref[...], preferred_element_type=F32)
    el = jnp.exp(logits - jnp.max(logits, axis=-1, keepdims=True))
    aff_ref[...] = el / jnp.sum(el, axis=-1, keepdims=True)


def post_mix(x, a, f, c, w_out, g_q, wq, kv, wo, g_f, w_router, seq_lens, tm=512):
    rows, d = x.shape
    n_mem = kv.shape[0] // len(seq_lens)
    n_exp = w_router.shape[1]
    tm = _row_tile(math.gcd(*seq_lens), tm)
    bounds = np.cumsum(seq_lens)[:-1] // tm

    def seq_of(i):
        s = jnp.int32(0)
        for b in bounds:
            s = s + (i >= int(b)).astype(jnp.int32)
        return s

    const = lambda i: (0, 0)
    row = lambda width: pl.BlockSpec((tm, width), lambda i: (i, 0))
    return pl.pallas_call(
        _post_mix_kernel,
        grid=(rows // tm,),
        in_specs=[
            row(d), row(a.shape[1]), row(f.shape[1]), row(c.shape[1]),
            pl.BlockSpec(w_out.shape, const),
            pl.BlockSpec((1, d), const),
            pl.BlockSpec(wq.shape, const),
            pl.BlockSpec((n_mem, kv.shape[1]), lambda i: (seq_of(i), 0)),
            pl.BlockSpec(wo.shape, const),
            pl.BlockSpec((1, d), const),
            pl.BlockSpec(w_router.shape, const),
        ],
        out_specs=[row(d), row(d), row(n_exp)],
        out_shape=[
            jax.ShapeDtypeStruct((rows, d), F32),
            jax.ShapeDtypeStruct((rows, d), F32),
            jax.ShapeDtypeStruct((rows, n_exp), F32),
        ],
        compiler_params=_params("parallel"),
        name="post_mix",
    )(x, a, f, c, w_out, g_q.reshape(1, d), wq, kv, wo, g_f.reshape(1, d), w_router)


LANES = 128


def _counts(m, tri, ones, lower):
    n_exp, c, _ = m.shape
    mb = m.astype(BF16).reshape(n_exp * c, LANES)
    within = jnp.dot(mb, tri, preferred_element_type=F32).reshape(n_exp, c, LANES)
    rowsum = jnp.dot(mb, ones, preferred_element_type=F32).reshape(n_exp, c, LANES)
    before = jnp.stack([jnp.dot(lower, rowsum[e].astype(BF16), preferred_element_type=F32)
                        for e in range(n_exp)], axis=0)
    return within, before


def _select_kernel(cap, a_ref, pos_ref, tok_ref, within_ref, start_ref):
    a = a_ref[...]
    n_exp, c, _ = a.shape
    bits = pltpu.bitcast(a, jnp.int32)
    count = lambda mask: jnp.sum(jnp.sum(mask.astype(F32), axis=1, keepdims=True), axis=2, keepdims=True)
    thr = jnp.zeros((n_exp, 1, 1), jnp.int32)
    for b in range(30, -1, -1):
        trial = thr | (1 << b)
        thr = jnp.where(count(bits >= trial) >= cap, trial, thr)
    above = bits > thr
    tied = bits == thr
    n_tied_taken = cap - count(above)
    r = lax.broadcasted_iota(jnp.int32, (LANES, LANES), 0)
    cidx = lax.broadcasted_iota(jnp.int32, (LANES, LANES), 1)
    tri = (r <= cidx).astype(F32).astype(BF16)
    ones = jnp.ones((LANES, LANES), BF16)
    lower = (lax.broadcasted_iota(jnp.int32, (c, c), 1) < lax.broadcasted_iota(jnp.int32, (c, c), 0))
    lower = lower.astype(F32).astype(BF16)
    tied_f = tied.astype(F32)
    tied_within, tied_before = _counts(tied_f, tri, ones, lower)
    tie_rank = tied_within - tied_f + tied_before
    sel = (above | (tied & (tie_rank < n_tied_taken))).astype(F32)
    within, before = _counts(sel, tri, ones, lower)
    pos_ref[...] = (within - sel + before).astype(jnp.int32)
    within_ref[...] = within
    start_ref[...] = before

    slot = lax.broadcasted_iota(jnp.int32, (cap, c), 0).astype(F32)
    row_id = lax.broadcasted_iota(jnp.int32, (cap, c), 1)
    slot_l = lax.broadcasted_iota(jnp.int32, (cap, LANES), 0).astype(F32)

    def invert(e, carry):
        starts = jnp.transpose(start_ref[e])[0:1, :]
        started = starts <= slot
        row_of = jnp.sum(started.astype(F32), axis=1, keepdims=True) - 1.0
        start_of = jnp.max(jnp.where(started, starts, 0.0), axis=1, keepdims=True)
        pick = (row_id == row_of.astype(jnp.int32)).astype(F32).astype(BF16)
        incl = jnp.dot(pick, within_ref[e].astype(BF16), preferred_element_type=F32)
        lane_of = jnp.sum((incl <= slot_l - start_of).astype(F32), axis=1, keepdims=True)
        tok_ref[e] = jnp.broadcast_to((row_of * LANES + lane_of).astype(jnp.int32), (cap, LANES))
        return carry

    lax.fori_loop(0, n_exp, invert, 0)


def route_select(aff_t, cap):
    n_exp, n = aff_t.shape
    assert n % (8 * LANES) == 0 and cap % 8 == 0
    c = n // LANES
    pos, tok = pl.pallas_call(
        functools.partial(_select_kernel, cap),
        out_shape=[jax.ShapeDtypeStruct((n_exp, c, LANES), jnp.int32),
                   jax.ShapeDtypeStruct((n_exp, cap, LANES), jnp.int32)],
        scratch_shapes=[pltpu.VMEM((n_exp, c, LANES), F32), pltpu.VMEM((n_exp, c, LANES), F32)],
        compiler_params=pltpu.CompilerParams(vmem_limit_bytes=V7X_VMEM_LIMIT_BYTES),
        name="route_select",
    )(aff_t.reshape(n_exp, c, LANES))
    return pos.reshape(n_exp, n), tok


GR_TM = 256


def _gather_kernel(gt, idx_ref, idx_next_ref, h_hbm, o_ref, buf, sems):
    i = pl.program_id(0)
    slot = i % 2

    def row_copy(tok, k, s):
        return pltpu.make_async_copy(h_hbm.at[pl.ds(tok, 1)], buf.at[s, pl.ds(k, 1)], sems.at[s])

    def start_tile(ref, s):
        def body(k, carry):
            row_copy(ref[0, 0, k], k, s).start()
            return carry
        lax.fori_loop(0, gt, body, 0, unroll=8)

    @pl.when(i == 0)
    def _():
        start_tile(idx_ref, slot)

    @pl.when(i + 1 < pl.num_programs(0))
    def _():
        start_tile(idx_next_ref, 1 - slot)

    pltpu.make_async_copy(h_hbm.at[pl.ds(0, gt)], buf.at[slot], sems.at[slot]).wait()
    o_ref[...] = buf[slot].astype(BF16)


def gather_rows(h, idx, gt=GR_TM):
    t, d = h.shape
    r = idx.shape[0]
    assert r % gt == 0
    n = r // gt
    idx3 = idx.reshape(n, 1, gt)
    smem = lambda f: pl.BlockSpec((1, 1, gt), f, memory_space=pltpu.SMEM)
    return pl.pallas_call(
        functools.partial(_gather_kernel, gt),
        grid=(n,),
        in_specs=[smem(lambda i: (i, 0, 0)), smem(lambda i: (jnp.minimum(i + 1, n - 1), 0, 0)),
                  pl.BlockSpec(memory_space=pl.ANY)],
        out_specs=pl.BlockSpec((gt, d), lambda i: (i, 0)),
        out_shape=jax.ShapeDtypeStruct((r, d), BF16),
        scratch_shapes=[pltpu.VMEM((2, gt, d), F32), pltpu.SemaphoreType.DMA((2,))],
        compiler_params=_params("arbitrary"),
        name="gather_rows",
    )(idx3, idx3, h)


def _expert_ffn_kernel(n_hidden_steps, fk, xe_ref, gate_ref, wg_ref, wu_ref, wd_ref, y_ref, hid_ref):
    s = pl.program_id(2)

    @pl.when(s < n_hidden_steps)
    def _():
        xe = xe_ref[0]
        a = jnp.dot(xe, wg_ref[0].astype(BF16), preferred_element_type=F32)
        b = jnp.dot(xe, wu_ref[0].astype(BF16), preferred_element_type=F32)
        hid_ref[:, pl.ds(pl.multiple_of(s * fk, fk), fk)] = (a * jax.nn.sigmoid(a) * b).astype(BF16)

    @pl.when(s >= n_hidden_steps)
    def _():
        y = jnp.dot(hid_ref[...], wd_ref[0].astype(BF16), preferred_element_type=F32)
        y_ref[0] = (y * gate_ref[0]).astype(y_ref.dtype)


def expert_ffn(xe, gates, w_gate, w_up, w_down, layer, tm=1024, fk=512, nk=256):
    n_exp, rows, d = xe.shape
    d_ff = w_gate.shape[-1]
    tm = _row_tile(rows, tm)
    fk = _row_tile(d_ff, fk)
    nk = _row_tile(d, nk)
    nf, nn = d_ff // fk, d // nk
    hid_step = lambda s: jnp.minimum(s, nf - 1)
    out_step = lambda s: jnp.maximum(s - nf, 0)
    return pl.pallas_call(
        functools.partial(_expert_ffn_kernel, nf, fk),
        grid=(n_exp, rows // tm, nf + nn),
        in_specs=[
            pl.BlockSpec((1, tm, d), lambda e, m, s: (e, m, 0)),
            pl.BlockSpec((1, tm, 1), lambda e, m, s: (e, m, 0)),
            pl.BlockSpec((None, 1, d, fk), lambda e, m, s: (layer, e, 0, hid_step(s))),
            pl.BlockSpec((None, 1, d, fk), lambda e, m, s: (layer, e, 0, hid_step(s))),
            pl.BlockSpec((None, 1, d_ff, nk), lambda e, m, s: (layer, e, 0, out_step(s))),
        ],
        out_specs=pl.BlockSpec((1, tm, nk), lambda e, m, s: (e, m, out_step(s))),
        out_shape=jax.ShapeDtypeStruct((n_exp, rows, d), BF16),
        scratch_shapes=[pltpu.VMEM((tm, d_ff), BF16)],
        compiler_params=_params("parallel", "parallel", "arbitrary"),
        name="expert_ffn",
    )(xe, gates, w_gate, w_up, w_down)


CB_TM = 256
CB_CH = 32
CB_KB = 256


def _combine_kernel(n_exp, tm, p0_ref, x2_ref, y_hbm, tok_hbm, o_ref, ybuf, tokbuf, ysem, toksem):
    i = pl.program_id(0)
    ch, kb = CB_CH, CB_KB

    @pl.when(i == 0)
    def _():
        ybuf[...] = jnp.zeros_like(ybuf)
        tokbuf[...] = jnp.full(tokbuf.shape, -1, jnp.int32)

    def y_copy(e, src, dst):
        return pltpu.make_async_copy(y_hbm.at[e, pl.ds(src, ch)], ybuf.at[pl.ds(dst, ch)], ysem)

    def tok_copy(e, src, dst):
        return pltpu.make_async_copy(tok_hbm.at[e, pl.ds(src, ch)], tokbuf.at[pl.ds(dst, ch)], toksem)

    off = jnp.int32(0)
    for e in range(n_exp):
        p0 = p0_ref[i, e]
        p1 = p0_ref[i + 1, e]
        a0 = (p0 // ch) * ch
        nch = jnp.where(p1 > p0, (p1 - a0 + ch - 1) // ch, 0)

        def issue(k, carry, e=e, a0=a0, off=off):
            src = pl.multiple_of(a0 + k * ch, ch)
            dst = pl.multiple_of(off + k * ch, ch)
            y_copy(e, src, dst).start()
            tok_copy(e, src, dst).start()
            return carry

        lax.fori_loop(0, nch, issue, 0)
        off = off + nch * ch

    def drain(k, carry):
        y_copy(0, 0, 0).wait()
        tok_copy(0, 0, 0).wait()
        return carry

    lax.fori_loop(0, off // ch, drain, 0)

    o_ref[...] = x2_ref[...]
    t0 = i * tm
    lane = lax.broadcasted_iota(jnp.int32, (kb, 128), 1)
    row = lax.broadcasted_iota(jnp.int32, (kb, 128), 0)

    def block(j, carry):
        r0 = pl.multiple_of(j * kb, kb)
        tok = jnp.where(row + r0 < off, tokbuf[pl.ds(r0, kb), :] - t0, -1)
        bt = jnp.concatenate([(tok == lane + 128 * c) for c in range(tm // 128)], axis=-1)
        bt = bt.astype(F32).astype(BF16)
        o_ref[...] += lax.dot_general(bt, ybuf[pl.ds(r0, kb), :], (((0,), (0,)), ((), ())),
                                      preferred_element_type=F32)
        return carry

    lax.fori_loop(0, (off + kb - 1) // kb, block, 0)


def moe_combine(x2, y, tok, p0, tm=CB_TM):
    t, d = x2.shape
    n_exp, r, _ = y.shape
    assert t % tm == 0 and r % CB_CH == 0 and tm % 128 == 0
    n_tiles = t // tm
    assert p0.shape == (n_tiles + 1, n_exp) and tok.shape == (n_exp, r, 128)
    buf_rows = pl.cdiv(n_exp * (tm + 2 * CB_CH), CB_KB) * CB_KB
    grid_spec = pltpu.PrefetchScalarGridSpec(
        num_scalar_prefetch=1,
        grid=(n_tiles,),
        in_specs=[pl.BlockSpec((tm, d), lambda i, p: (i, 0)),
                  pl.BlockSpec(memory_space=pl.ANY),
                  pl.BlockSpec(memory_space=pl.ANY)],
        out_specs=pl.BlockSpec((tm, d), lambda i, p: (i, 0)),
        scratch_shapes=[pltpu.VMEM((buf_rows, d), BF16), pltpu.VMEM((buf_rows, 128), jnp.int32),
                        pltpu.SemaphoreType.DMA(()), pltpu.SemaphoreType.DMA(())],
    )
    return pl.pallas_call(
        functools.partial(_combine_kernel, n_exp, tm),
        grid_spec=grid_spec,
        out_shape=jax.ShapeDtypeStruct((t, d), F32),
        compiler_params=_params("arbitrary"),
        name="moe_combine",
    )(p0, x2, y, tok)


def _final_norm_kernel(x_ref, g_ref, o_ref):
    o_ref[...] = _rms(x_ref[...], g_ref[...])


def final_norm(x, g, row0, rows, tm=1024):
    d = x.shape[1]
    tm = _row_tile(math.gcd(rows, row0) if row0 else rows, tm)
    blk0 = row0 // tm
    return pl.pallas_call(
        _final_norm_kernel,
        grid=(rows // tm,),
        in_specs=[pl.BlockSpec((tm, d), lambda i: (i + blk0, 0)), pl.BlockSpec((1, d), lambda i: (0, 0))],
        out_specs=pl.BlockSpec((tm, d), lambda i: (i, 0)),
        out_shape=jax.ShapeDtypeStruct((rows, d), F32),
        compiler_params=_params("parallel"),
        name="final_norm",
    )(x, g.reshape(1, d))


def kernel(x_prompt, x_sample, mem_prompt, mem_sample, w_in, w_out, na_bias, fnet_w, sink_logit,
           wq_mem, wkv_mem, wo_mem, w_router, w_gate, w_up, w_down,
           g_mix, g_mem_q, g_mem_kv, g_ffn, g_final):
    d = x_prompt.shape[-1]
    depth = w_in.shape[0]
    n_exp = w_router.shape[-1]
    groups = [(x_prompt.shape[0], x_prompt.shape[1]), (x_sample.shape[0], x_sample.shape[1])]
    seq_lens = [s for b, s in groups for _ in range(b)]
    group_tokens = [b * s for b, s in groups]
    caps = [EC_CAPACITY_FACTOR * n // n_exp for n in group_tokens]
    offs = np.concatenate([[0], np.cumsum(group_tokens)]).tolist()

    x = jnp.concatenate([x_prompt.reshape(-1, d), x_sample.reshape(-1, d)], axis=0)
    mem = jnp.concatenate([mem_prompt.reshape(-1, d), mem_sample.reshape(-1, d)], axis=0)

    w_in_b, w_out_b = w_in.astype(BF16), w_out.astype(BF16)
    wq_b, wkv_b, wo_b = wq_mem.astype(BF16), wkv_mem.astype(BF16), wo_mem.astype(BF16)
    wr_b, fnet_b = w_router.astype(BF16), fnet_w.astype(BF16)

    for l in range(depth):
        z = norm_matmul(x, g_mix[l], w_in_b[l], BF16)
        a = na_attention(z, na_bias[l], seq_lens)
        c = swa_attention(z, sink_logit[l], seq_lens)
        u = z[:, Z_FN:Z_FN + FN_WIDTH]
        f = jnp.concatenate(
            [fnet_mix(u[offs[gi]:offs[gi + 1]].reshape(b, s, FN_WIDTH), fnet_b[l]).reshape(b * s, FN_WIDTH)
             for gi, (b, s) in enumerate(groups)], axis=0)
        kv = norm_matmul(mem, g_mem_kv[l], wkv_b[l], BF16)
        x2, h3, aff = post_mix(x, a, f, c, w_out_b[l], g_mem_q[l], wq_b[l], kv, wo_b[l], g_ffn[l],
                               wr_b[l], seq_lens)
        aff_t = aff.T
        toks, p0s, slot0 = [], [], 0
        for gi in range(len(groups)):
            pos, tok_g = route_select(aff_t[:, offs[gi]:offs[gi + 1]], caps[gi])
            toks.append(tok_g + offs[gi])
            p0s.append(pos[:, ::CB_TM].T + slot0)
            slot0 += caps[gi]
        tok = jnp.concatenate(toks, axis=1)
        idx = tok[:, :, 0]
        p0 = jnp.concatenate(p0s + [jnp.full((1, n_exp), slot0, jnp.int32)], axis=0)
        gate = jnp.take_along_axis(aff_t, idx, axis=1)[..., None]
        xe = gather_rows(h3, idx.reshape(-1)).reshape(n_exp, slot0, d)
        y = expert_ffn(xe, gate, w_gate, w_up, w_down, l)
        x = moe_combine(x2, y, tok, p0)

    y_prompt = final_norm(x, g_final, offs[0], group_tokens[0]).reshape(x_prompt.shape)
    y_sample = final_norm(x, g_final, offs[1], group_tokens[1]).reshape(x_sample.shape)
    return (y_prompt, y_sample)
```
